```python
import jax, jax.numpy as jnp
from jax import lax
import numpy as np

D_MODEL = 1024
BATCH = 4
SEQ = 8192
DEPTH = 2

MEM_LEN = 256
N_EVEN = (DEPTH + 1) // 2
N_ODD = DEPTH // 2
EPS = 1e-6

A_HEADS = 4
A_WIDTH = D_MODEL // 2
HEAD_DIM = A_WIDTH // A_HEADS
IDX_HEADS = 16
IDX_DIM = 64
TOPK_MAX = 256
Q_BLOCK = 128
ROPE_THETA = 500000.0

POOL_WINDOWS = (2, 4, 8, 16)
B_WIDTH = D_MODEL - A_WIDTH
POOL_GROUP = B_WIDTH // len(POOL_WINDOWS)

AB_SIZES = (A_WIDTH, A_WIDTH, A_WIDTH, B_WIDTH, IDX_HEADS * IDX_DIM, IDX_HEADS, IDX_DIM)
IN_AB = sum(AB_SIZES)
AB_SPLITS = [int(s) for s in np.cumsum(AB_SIZES)[:-1]]

D_CONV = D_MODEL
CONV_WIDTH = 31

C_HEADS = 4
C_HEAD_DIM = D_MODEL // C_HEADS

D_FF = -(-8 * D_MODEL // (3 * 256)) * 256

kernel_name = 'hybrid_dsa_pool_conformer_trunk'


def rmsnorm(x, g):
    xf = x.astype(jnp.float32)
    y = xf * lax.rsqrt(jnp.mean(xf * xf, axis=-1, keepdims=True) + EPS)
    return (y * g.astype(jnp.float32)).astype(x.dtype)


def layernorm(x, g, b):
    xf = x.astype(jnp.float32)
    mu = jnp.mean(xf, axis=-1, keepdims=True)
    var = jnp.mean(jnp.square(xf - mu), axis=-1, keepdims=True)
    y = (xf - mu) * lax.rsqrt(var + EPS)
    return (y * g.astype(jnp.float32) + b.astype(jnp.float32)).astype(x.dtype)


def rope_partial(x, pos):
    d = x.shape[-1]
    rot = d // 4
    half = rot // 2
    inv = ROPE_THETA ** (-jnp.arange(half, dtype=jnp.float32) * 2.0 / rot)
    ang = pos.astype(jnp.float32)[:, None] * inv[None, :]
    cos = jnp.cos(ang)[:, None, :]
    sin = jnp.sin(ang)[:, None, :]
    xf = x.astype(jnp.float32)
    x1, x2, rest = xf[..., :half], xf[..., half:rot], xf[..., rot:]
    out = jnp.concatenate([x1 * cos - x2 * sin, x2 * cos + x1 * sin, rest], axis=-1)
    return out.astype(x.dtype)


def dsa_attention(q, k, v, iq, iw, ik):
    bsz, t_len = q.shape[0], q.shape[1]
    topk = min(TOPK_MAX, t_len // 4)
    n_blocks = t_len // Q_BLOCK
    key_pos = jnp.arange(t_len)
    bidx = jnp.arange(bsz)[:, None, None]
    scale = HEAD_DIM ** -0.5

    def block(i):
        start = i * Q_BLOCK
        qb = lax.dynamic_slice_in_dim(q, start, Q_BLOCK, axis=1)
        iqb = lax.dynamic_slice_in_dim(iq, start, Q_BLOCK, axis=1)
        iwb = lax.dynamic_slice_in_dim(iw, start, Q_BLOCK, axis=1)
        qpos = start + jnp.arange(Q_BLOCK)
        s = jnp.einsum('bqhd,bsd->bqhs', iqb, ik).astype(jnp.float32)
        scores = jnp.einsum('bqhs,bqh->bqs', jax.nn.relu(s), iwb.astype(jnp.float32))
        causal = key_pos[None, :] <= qpos[:, None]
        scores = jnp.where(causal[None], scores, -jnp.inf)
        _, idx = lax.top_k(scores, topk)
        valid = idx <= qpos[None, :, None]
        ks = k[bidx, idx]
        vs = v[bidx, idx]
        logits = jnp.einsum('bqhd,bqkhd->bhqk', qb, ks).astype(jnp.float32) * scale
        logits = jnp.where(valid[:, None], logits, -jnp.inf)
        p = jax.nn.softmax(logits, axis=-1).astype(v.dtype)
        return jnp.einsum('bhqk,bqkhd->bqhd', p, vs)

    out = lax.map(block, jnp.arange(n_blocks))
    return jnp.transpose(out, (1, 0, 2, 3, 4)).reshape(q.shape)


def multiscale_pool(u, pool_w):
    t_len = u.shape[1]
    uf = u.astype(jnp.float32)
    cs = jnp.cumsum(uf, axis=1)
    t = jnp.arange(t_len)
    outs = []
    for g, w in enumerate(POOL_WINDOWS):
        sl = slice(g * POOL_GROUP, (g + 1) * POOL_GROUP)
        c = cs[..., sl]
        shifted = jnp.pad(c, ((0, 0), (w, 0), (0, 0)))[:, :t_len]
        cnt = jnp.minimum(t + 1, w).astype(jnp.float32)[None, :, None]
        outs.append((c - shifted) / cnt - uf[..., sl])
    pooled = jnp.stack(outs, axis=2).astype(u.dtype)
    y = jnp.einsum('btgc,gcd->btgd', pooled, pool_w)
    return y.reshape(u.shape)


def mixer_ab(xn, w_in, q_gain, k_gain, pool_w, pool_scale, w_out):
    bsz, t_len, _ = xn.shape
    proj = xn @ w_in
    q, k, v, u, iq, iw, ik = jnp.split(proj, AB_SPLITS, axis=-1)
    pos = jnp.arange(t_len)
    q = rope_partial(rmsnorm(q.reshape(bsz, t_len, A_HEADS, HEAD_DIM), q_gain), pos)
    k = rope_partial(rmsnorm(k.reshape(bsz, t_len, A_HEADS, HEAD_DIM), k_gain), pos)
    v = v.reshape(bsz, t_len, A_HEADS, HEAD_DIM)
    iq = rope_partial(iq.reshape(bsz, t_len, IDX_HEADS, IDX_DIM), pos)
    ik = rope_partial(ik.reshape(bsz, t_len, 1, IDX_DIM), pos)[:, :, 0]
    iw = iw * (IDX_HEADS ** -0.5 * IDX_DIM ** -0.5)
    a = dsa_attention(q, k, v, iq, iw, ik).reshape(bsz, t_len, A_WIDTH)
    b = multiscale_pool(u, pool_w) * pool_scale
    return jnp.concatenate([a, b], axis=-1) @ w_out


def conv_module(xn, w_in, b_in, dw_w, dw_b, ln_g, ln_b, w_out):
    h = xn @ w_in + b_in
    a, gate = jnp.split(h, 2, axis=-1)
    h = a * jax.nn.sigmoid(gate)
    h = lax.conv_general_dilated(
        h, dw_w[:, None, :], window_strides=(1,), padding=[(CONV_WIDTH - 1, 0)],
        dimension_numbers=('NWC', 'WIO', 'NWC'), feature_group_count=D_CONV) + dw_b
    h = jax.nn.silu(layernorm(h, ln_g, ln_b))
    return h @ w_out


def cross_attn(hn, memn, wq, wk, wv, q_gain, k_gain, wo):
    bsz, t_len, _ = hn.shape
    m_len = memn.shape[1]
    q = rmsnorm((hn @ wq).reshape(bsz, t_len, C_HEADS, C_HEAD_DIM), q_gain)
    k = rmsnorm((memn @ wk).reshape(bsz, m_len, C_HEADS, C_HEAD_DIM), k_gain)
    v = (memn @ wv).reshape(bsz, m_len, C_HEADS, C_HEAD_DIM)
    logits = jnp.einsum('bthd,bmhd->bhtm', q, k).astype(jnp.float32) * C_HEAD_DIM ** -0.5
    p = jax.nn.softmax(logits, axis=-1).astype(v.dtype)
    o = jnp.einsum('bhtm,bmhd->bthd', p, v).reshape(bsz, t_len, D_MODEL)
    return o @ wo


def swiglu(xn, w_gate, w_up, w_down):
    return (jax.nn.silu(xn @ w_gate) * (xn @ w_up)) @ w_down


def setup_inputs(seed: int = 0) -> dict:
    key = jax.random.key(seed)
    ks = iter(jax.random.split(key, 40))

    def dense(shape, fan_in):
        return jax.random.normal(next(ks), shape, jnp.float32) * fan_in ** -0.5

    def gain(shape):
        return 1.0 + 0.05 * jax.random.normal(next(ks), shape, jnp.float32)

    def bias(shape):
        return 0.02 * jax.random.normal(next(ks), shape, jnp.float32)

    return {
        'x': jax.random.normal(next(ks), (BATCH, SEQ, D_MODEL), jnp.float32),
        'mem': jax.random.normal(next(ks), (BATCH, MEM_LEN, D_MODEL), jnp.float32),
        'norm_mix': gain((DEPTH, D_MODEL)),
        'norm_cross': gain((DEPTH, D_MODEL)),
        'norm_mem': gain((DEPTH, D_MODEL)),
        'norm_ffn': gain((DEPTH, D_MODEL)),
        'w_in_ab': dense((N_EVEN, D_MODEL, IN_AB), D_MODEL),
        'a_q_norm': gain((N_EVEN, HEAD_DIM)),
        'a_k_norm': gain((N_EVEN, HEAD_DIM)),
        'pool_w': dense((N_EVEN, len(POOL_WINDOWS), POOL_GROUP, POOL_GROUP), POOL_GROUP),
        'pool_scale': gain((N_EVEN, B_WIDTH)),
        'w_out_ab': dense((N_EVEN, D_MODEL, D_MODEL), D_MODEL),
        'conv_w_in': dense((N_ODD, D_MODEL, 2 * D_CONV), D_MODEL),
        'conv_b_in': bias((N_ODD, 2 * D_CONV)),
        'conv_dw_w': dense((N_ODD, CONV_WIDTH, D_CONV), CONV_WIDTH),
        'conv_dw_b': bias((N_ODD, D_CONV)),
        'conv_ln_g': gain((N_ODD, D_CONV)),
        'conv_ln_b': bias((N_ODD, D_CONV)),
        'conv_w_out': dense((N_ODD, D_CONV, D_MODEL), D_CONV),
        'cross_wq': dense((DEPTH, D_MODEL, D_MODEL), D_MODEL),
        'cross_wk': dense((DEPTH, D_MODEL, D_MODEL), D_MODEL),
        'cross_wv': dense((DEPTH, D_MODEL, D_MODEL), D_MODEL),
        'cross_q_norm': gain((DEPTH, C_HEAD_DIM)),
        'cross_k_norm': gain((DEPTH, C_HEAD_DIM)),
        'cross_wo': dense((DEPTH, D_MODEL, D_MODEL), D_MODEL),
        'ffn_w_gate': dense((DEPTH, D_MODEL, D_FF), D_MODEL),
        'ffn_w_up': dense((DEPTH, D_MODEL, D_FF), D_MODEL),
        'ffn_w_down': dense((DEPTH, D_FF, D_MODEL), D_FF),
    }


def reference(x, mem, norm_mix, norm_cross, norm_mem, norm_ffn, w_in_ab, a_q_norm, a_k_norm,
              pool_w, pool_scale, w_out_ab, conv_w_in, conv_b_in, conv_dw_w, conv_dw_b,
              conv_ln_g, conv_ln_b, conv_w_out, cross_wq, cross_wk, cross_wv, cross_q_norm,
              cross_k_norm, cross_wo, ffn_w_gate, ffn_w_up, ffn_w_down):
    h = x
    for l in range(DEPTH):
        xn = rmsnorm(h, norm_mix[l])
        if l % 2 == 0:
            e = l // 2
            h = h + mixer_ab(xn, w_in_ab[e], a_q_norm[e], a_k_norm[e], pool_w[e],
                             pool_scale[e], w_out_ab[e])
        else:
            o = l // 2
            h = h + conv_module(xn, conv_w_in[o], conv_b_in[o], conv_dw_w[o], conv_dw_b[o],
                                conv_ln_g[o], conv_ln_b[o], conv_w_out[o])
        h = h + cross_attn(rmsnorm(h, norm_cross[l]), rmsnorm(mem, norm_mem[l]),
                           cross_wq[l], cross_wk[l], cross_wv[l], cross_q_norm[l],
                           cross_k_norm[l], cross_wo[l])
        h = h + swiglu(rmsnorm(h, norm_ffn[l]), ffn_w_gate[l], ffn_w_up[l], ffn_w_down[l])
    return h
```

```python
import functools

import jax
import jax.numpy as jnp
from jax import lax
from jax.experimental import pallas as pl
from jax.experimental.pallas import tpu as pltpu

F32 = jnp.float32
BF16 = jnp.bfloat16

EPS = 1e-6
A_HEADS = 4
HEAD_DIM = 128
IDX_HEADS = 16
IDX_DIM = 64
TOPK_MAX = 256
ROPE_THETA = 500000.0
POOL_WINDOWS = (2, 4, 8, 16)
POOL_GROUP = 128
POOL_HIST = 16
C_HEADS = 4
CONV_WIDTH = 31
CONV_HIST = 32

VMEM_LIMIT_BYTES = 56 * 1024 * 1024
SEQ_CHUNK = 256
ROW_TILE = 512
INT32_MIN = -2 ** 31
MASKED_LOGIT = -1e30


def _params(n_axes):
    return pltpu.CompilerParams(
        dimension_semantics=("arbitrary",) * n_axes,
        vmem_limit_bytes=VMEM_LIMIT_BYTES)


def _resident(shape):
    zeros = (0,) * len(shape)
    return pl.BlockSpec(shape, lambda *_: zeros, pipeline_mode=pl.Buffered(1))


def _rms_rows(x, g):
    ms = jnp.mean(x * x, axis=-1, keepdims=True)
    return x * lax.rsqrt(ms + EPS) * g


def _bdot(a, b):
    return jnp.dot(a, b, preferred_element_type=F32)


def _rope_cols(y, cos, sin, half):
    y1, y2 = y[0:half], y[half:2 * half]
    return jnp.concatenate(
        [y1 * cos - y2 * sin, y2 * cos + y1 * sin, y[2 * half:]], axis=0)


def _front_kernel(x_ref, g_ref, wt_ref, wu_ref, gq_ref, gk_ref, cos16_ref, sin16_ref,
                  cos8_ref, sin8_ref, poolw_ref, pscale_ref,
                  qT_ref, k_ref, vT_ref, iqT_ref, ik_ref, iwT_ref, b_ref, ubuf_ref,
                  *, tiles_per_seq):
    tm = x_ref.shape[0]
    i = pl.program_id(0)
    aw = A_HEADS * HEAD_DIM
    iqw = IDX_HEADS * IDX_DIM

    xn = _rms_rows(x_ref[...], g_ref[...])
    xb = xn.astype(BF16)
    xT = xn.T.astype(BF16)
    cols = _bdot(wt_ref[...], xT)

    cos16, sin16 = cos16_ref[...], sin16_ref[...]
    cos8, sin8 = cos8_ref[...], sin8_ref[...]

    def qk_heads(base, gain):
        heads = []
        for h in range(A_HEADS):
            xh = cols[base + h * HEAD_DIM: base + (h + 1) * HEAD_DIM]
            ms = jnp.sum(xh * xh, axis=0, keepdims=True) * (1.0 / HEAD_DIM)
            y = xh * lax.rsqrt(ms + EPS) * gain
            heads.append(_rope_cols(y, cos16, sin16, HEAD_DIM // 8))
        return jnp.concatenate(heads, axis=0)

    qT_ref[0] = qk_heads(0, gq_ref[...]).astype(BF16)
    k_ref[0] = qk_heads(aw, gk_ref[...]).T.astype(BF16)
    vT_ref[0, 0] = cols[2 * aw:3 * aw].astype(BF16)

    base = 3 * aw
    iq_heads = [
        _rope_cols(cols[base + h * IDX_DIM: base + (h + 1) * IDX_DIM], cos8, sin8, IDX_DIM // 8)
        for h in range(IDX_HEADS)]
    iqT_ref[0] = jnp.concatenate(iq_heads, axis=0).astype(BF16)

    base = 3 * aw + iqw
    ikT = _rope_cols(cols[base:base + IDX_DIM], cos8, sin8, IDX_DIM // 8)
    ikT = jnp.concatenate([ikT, jnp.zeros_like(ikT)], axis=0)
    ik_ref[0] = ikT.T[:, :IDX_DIM].astype(BF16)

    base = 3 * aw + iqw + IDX_DIM
    iwT_ref[0] = cols[base:base + IDX_HEADS] * (IDX_HEADS ** -0.5 * IDX_DIM ** -0.5)

    u = _bdot(xb, wu_ref[...])
    t_in_seq = (i % tiles_per_seq) * tm

    @pl.when(i % tiles_per_seq == 0)
    def _():
        ubuf_ref[0:POOL_HIST, :] = jnp.zeros((POOL_HIST, u.shape[1]), F32)

    ubuf_ref[POOL_HIST:POOL_HIST + tm, :] = u
    t1 = (lax.broadcasted_iota(jnp.int32, (tm, 1), 0) + t_in_seq + 1).astype(F32)
    outs = []
    for g, w in enumerate(POOL_WINDOWS):
        cs = slice(g * POOL_GROUP, (g + 1) * POOL_GROUP)
        acc = u[:, cs]
        for j in range(1, w):
            acc = acc + ubuf_ref[POOL_HIST - j:POOL_HIST - j + tm, cs]
        pooled = acc / jnp.minimum(t1, float(w)) - u[:, cs]
        outs.append(_bdot(pooled.astype(BF16), poolw_ref[g]))
    b_ref[...] = (jnp.concatenate(outs, axis=-1) * pscale_ref[...]).astype(BF16)
    ubuf_ref[0:POOL_HIST, :] = u[tm - POOL_HIST:tm, :]


def _front(x2, g, wt, wu, gq, gk, cos16, sin16, cos8, sin8, poolw, pscale, *, B, T):
    N, D = x2.shape
    tm = SEQ_CHUNK
    tps = T // tm
    aw = A_HEADS * HEAD_DIM
    iqw = IDX_HEADS * IDX_DIM
    pw = len(POOL_WINDOWS) * POOL_GROUP
    seq_map = lambda i: (i // tps, 0, i % tps)
    out_shape = (
        jax.ShapeDtypeStruct((B, aw, T), BF16),
        jax.ShapeDtypeStruct((B, T, aw), BF16),
        jax.ShapeDtypeStruct((B, tps, aw, tm), BF16),
        jax.ShapeDtypeStruct((B, iqw, T), BF16),
        jax.ShapeDtypeStruct((B, T, IDX_DIM), BF16),
        jax.ShapeDtypeStruct((B, IDX_HEADS, T), F32),
        jax.ShapeDtypeStruct((N, pw), BF16),
    )
    return pl.pallas_call(
        functools.partial(_front_kernel, tiles_per_seq=tps),
        grid=(N // tm,),
        in_specs=[
            pl.BlockSpec((tm, D), lambda i: (i, 0)),
            _resident(g.shape), _resident(wt.shape), _resident(wu.shape),
            _resident(gq.shape), _resident(gk.shape),
            pl.BlockSpec((HEAD_DIM // 8, tm), lambda i: (0, i % tps)),
            pl.BlockSpec((HEAD_DIM // 8, tm), lambda i: (0, i % tps)),
            pl.BlockSpec((IDX_DIM // 8, tm), lambda i: (0, i % tps)),
            pl.BlockSpec((IDX_DIM // 8, tm), lambda i: (0, i % tps)),
            _resident(poolw.shape), _resident(pscale.shape),
        ],
        out_specs=(
            pl.BlockSpec((1, aw, tm), seq_map),
            pl.BlockSpec((1, tm, aw), lambda i: (i // tps, i % tps, 0)),
            pl.BlockSpec((1, 1, aw, tm), lambda i: (i // tps, i % tps, 0, 0)),
            pl.BlockSpec((1, iqw, tm), seq_map),
            pl.BlockSpec((1, tm, IDX_DIM), lambda i: (i // tps, i % tps, 0)),
            pl.BlockSpec((1, IDX_HEADS, tm), seq_map),
            pl.BlockSpec((tm, pw), lambda i: (i, 0)),
        ),
        out_shape=out_shape,
        scratch_shapes=[pltpu.VMEM((POOL_HIST + tm, pw), F32)],
        compiler_params=_params(1),
        name="front",
    )(x2, g, wt, wu, gq, gk, cos16, sin16, cos8, sin8, poolw, pscale)


def _key_to_f32(c):
    bits = c ^ ((c >> 31) & 0x7FFFFFFF)
    return lax.bitcast_convert_type(bits, F32)


def _dsa_kernel(iqT_ref, iwT_ref, qT_ref, ik_ref, k_ref, vT_ref, o_ref, s_ref, acc_ref,
                *, topk):
    ch = SEQ_CHUNK
    qi = pl.program_id(1)
    nblk = qi + 1

    row_i = lax.broadcasted_iota(jnp.int32, (ch, ch), 0)
    col_i = lax.broadcasted_iota(jnp.int32, (ch, ch), 1)
    for h in range(IDX_HEADS):
        w_h = iwT_ref[0, h:h + 1, :]
        rhs = iqT_ref[0, h * IDX_DIM:(h + 1) * IDX_DIM, :]

        def score_body(j, carry, h=h, w_h=w_h, rhs=rhs):
            s0 = pl.multiple_of(j * ch, ch)
            val = jnp.maximum(_bdot(ik_ref[0, pl.ds(s0, ch), :], rhs), 0.0) * w_h
            if h > 0:
                val = s_ref[pl.ds(s0, ch), :] + val
            if h == IDX_HEADS - 1:
                val = jnp.where(row_i + s0 <= col_i + qi * ch, val, -jnp.inf)
            s_ref[pl.ds(s0, ch), :] = val
            return carry

        lax.fori_loop(0, nblk, score_body, 0)

    def count_ge(cand):
        def body(j, acc):
            s0 = pl.multiple_of(j * ch, ch)
            m = jnp.where(s_ref[pl.ds(s0, ch), :] >= cand, 1.0, 0.0)
            part = m[0:8]
            for r in range(1, ch // 8):
                part = part + m[8 * r:8 * r + 8]
            return acc + part

        acc = lax.fori_loop(0, nblk, body, jnp.zeros((8, ch), F32))
        return jnp.sum(acc, axis=0, keepdims=True)

    kf = float(topk)
    c = jnp.where(count_ge(jnp.zeros((1, ch), F32)) >= kf, 0, INT32_MIN).astype(jnp.int32)

    def bit_body(i, c):
        trial = c + jnp.left_shift(jnp.int32(1), 30 - i)
        return jnp.where(count_ge(_key_to_f32(trial)) >= kf, trial, c)

    c = lax.fori_loop(0, 31, bit_body, c)
    thr = jnp.where(c == INT32_MIN, -3.0e38, _key_to_f32(c))

    scale = HEAD_DIM ** -0.5
    acc_ref[...] = jnp.zeros(acc_ref.shape, F32)

    def attn_body(j, carry):
        ms, ls = carry
        s0 = pl.multiple_of(j * ch, ch)
        sel = s_ref[pl.ds(s0, ch), :] >= thr
        new_ms, new_ls = [], []
        for h in range(A_HEADS):
            hs = slice(h * HEAD_DIM, (h + 1) * HEAD_DIM)
            lg = _bdot(k_ref[0, pl.ds(s0, ch), hs], qT_ref[0, hs, :]) * scale
            lg = jnp.where(sel, lg, MASKED_LOGIT)
            m_new = jnp.maximum(ms[h], jnp.max(lg, axis=0, keepdims=True))
            alpha = jnp.exp(ms[h] - m_new)
            p = jnp.exp(lg - m_new)
            new_ls.append(alpha * ls[h] + jnp.sum(p, axis=0, keepdims=True))
            new_ms.append(m_new)
            acc_ref[hs, :] = alpha * acc_ref[hs, :] + _bdot(vT_ref[0, j, hs, :], p.astype(BF16))
        return tuple(new_ms), tuple(new_ls)

    init = (tuple(jnp.full((1, ch), MASKED_LOGIT, F32) for _ in range(A_HEADS)),
            tuple(jnp.zeros((1, ch), F32) for _ in range(A_HEADS)))
    _, ls = lax.fori_loop(0, nblk, attn_body, init)
    outT = jnp.concatenate(
        [acc_ref[h * HEAD_DIM:(h + 1) * HEAD_DIM, :] * (1.0 / ls[h]) for h in range(A_HEADS)],
        axis=0)
    o_ref[0] = outT.T.astype(BF16)


def _dsa(qT, k, vT, iqT, ik, iwT, *, B, T):
    ch = SEQ_CHUNK
    aw = A_HEADS * HEAD_DIM
    iqw = IDX_HEADS * IDX_DIM
    topk = min(TOPK_MAX, T // 4)
    blk = lambda b, q: (b, 0, q)
    per_batch = lambda shape: pl.BlockSpec(
        (1,) + shape, lambda b, q: (b,) + (0,) * len(shape), pipeline_mode=pl.Buffered(1))
    return pl.pallas_call(
        functools.partial(_dsa_kernel, topk=topk),
        grid=(B, T // ch),
        in_specs=[
            pl.BlockSpec((1, iqw, ch), blk),
            pl.BlockSpec((1, IDX_HEADS, ch), blk),
            pl.BlockSpec((1, aw, ch), blk),
            per_batch((T, IDX_DIM)),
            per_batch((T, aw)),
            per_batch((T // ch, aw, ch)),
        ],
        out_specs=pl.BlockSpec((1, ch, aw), lambda b, q: (b, q, 0)),
        out_shape=jax.ShapeDtypeStruct((B, T, aw), BF16),
        scratch_shapes=[pltpu.VMEM((T, ch), F32), pltpu.VMEM((aw, ch), F32)],
        compiler_params=_params(2),
        name="dsa",
    )(iqT, iwT, qT, ik, k, vT)


def _mixout_kernel(x_ref, a_ref, b_ref, wa_ref, wb_ref, o_ref):
    o_ref[...] = x_ref[...] + _bdot(a_ref[...], wa_ref[...]) + _bdot(b_ref[...], wb_ref[...])


def _mixout(x2, a2, b2, wa, wb):
    N, D = x2.shape
    tm = ROW_TILE
    row = lambda w: pl.BlockSpec((tm, w), lambda i: (i, 0))
    return pl.pallas_call(
        _mixout_kernel,
        grid=(N // tm,),
        in_specs=[row(D), row(a2.shape[1]), row(b2.shape[1]), _resident(wa.shape), _resident(wb.shape)],
        out_specs=row(D),
        out_shape=jax.ShapeDtypeStruct((N, D), F32),
        compiler_params=_params(1),
        name="mixout",
    )(x2, a2, b2, wa, wb)


def _mem_kernel(mem_ref, g_ref, wk_ref, wv_ref, gk_ref, kT_ref, v_ref):
    dh = gk_ref.shape[-1]
    mn = _rms_rows(mem_ref[0], g_ref[0]).astype(BF16)
    k = _bdot(mn, wk_ref[0])
    v_ref[0, 0] = _bdot(mn, wv_ref[0]).astype(BF16)
    for c in range(C_HEADS):
        kc = _rms_rows(k[:, c * dh:(c + 1) * dh], gk_ref[0])
        kT_ref[0, 0, c] = kc.T.astype(BF16)


def _mem_kv(mem, g, wk, wv, gk):
    B, M, D = mem.shape
    L = wk.shape[0]
    dh = D // C_HEADS
    lay = lambda *tail: pl.BlockSpec((1,) + tail, lambda l, b: (l,) + (0,) * len(tail))
    return pl.pallas_call(
        _mem_kernel,
        grid=(L, B),
        in_specs=[
            pl.BlockSpec((1, M, D), lambda l, b: (b, 0, 0)),
            lay(1, D), lay(D, D), lay(D, D), lay(1, dh),
        ],
        out_specs=(
            pl.BlockSpec((1, 1, C_HEADS, dh, M), lambda l, b: (l, b, 0, 0, 0)),
            pl.BlockSpec((1, 1, M, D), lambda l, b: (l, b, 0, 0)),
        ),
        out_shape=(
            jax.ShapeDtypeStruct((L, B, C_HEADS, dh, M), BF16),
            jax.ShapeDtypeStruct((L, B, M, D), BF16),
        ),
        compiler_params=_params(2),
        name="mem_kv",
    )(mem, g, wk, wv, gk)


def _cross_kernel(h_ref, g_ref, wq_ref, gq_ref, kT_ref, v_ref, wo_ref, o_ref):
    dh = gq_ref.shape[-1]
    h = h_ref[...]
    q = _bdot(_rms_rows(h, g_ref[...]).astype(BF16), wq_ref[...])
    heads = []
    for c in range(C_HEADS):
        cs = slice(c * dh, (c + 1) * dh)
        qc = _rms_rows(q[:, cs], gq_ref[...]).astype(BF16)
        lg = _bdot(qc, kT_ref[0, c]) * dh ** -0.5
        p = jnp.exp(lg - jnp.max(lg, axis=-1, keepdims=True))
        p = p * (1.0 / jnp.sum(p, axis=-1, keepdims=True))
        heads.append(_bdot(p.astype(BF16), v_ref[0, :, cs]))
    o = jnp.concatenate(heads, axis=-1).astype(BF16)
    o_ref[...] = h + _bdot(o, wo_ref[...])


def _cross(h2, g, wq, gq, kT, v, wo, *, T):
    N, D = h2.shape
    tm = ROW_TILE
    tps = T // tm
    dh = D // C_HEADS
    M = v.shape[1]
    return pl.pallas_call(
        _cross_kernel,
        grid=(N // tm,),
        in_specs=[
            pl.BlockSpec((tm, D), lambda i: (i, 0)),
            _resident(g.shape), _resident(wq.shape), _resident(gq.shape),
            pl.BlockSpec((1, C_HEADS, dh, M), lambda i: (i // tps, 0, 0, 0)),
            pl.BlockSpec((1, M, D), lambda i: (i // tps, 0, 0)),
            _resident(wo.shape),
        ],
        out_specs=pl.BlockSpec((tm, D), lambda i: (i, 0)),
        out_shape=jax.ShapeDtypeStruct((N, D), F32),
        compiler_params=_params(1),
        name="cross",
    )(h2, g, wq, gq, kT, v, wo)


def _ffn_kernel(h_ref, g_ref, wg_ref, wu_ref, wd_ref, o_ref, *, n_chunks):
    h = h_ref[...]
    hn = _rms_rows(h, g_ref[...]).astype(BF16)
    fc = wg_ref.shape[1] // n_chunks
    acc = h
    for c in range(n_chunks):
        cs = slice(c * fc, (c + 1) * fc)
        gate = _bdot(hn, wg_ref[:, cs])
        act = gate * jax.nn.sigmoid(gate) * _bdot(hn, wu_ref[:, cs])
        acc = acc + _bdot(act.astype(BF16), wd_ref[cs, :])
    o_ref[...] = acc


def _ffn(h2, g, wg, wu, wd):
    N, D = h2.shape
    tm = ROW_TILE
    return pl.pallas_call(
        functools.partial(_ffn_kernel, n_chunks=2),
        grid=(N // tm,),
        in_specs=[
            pl.BlockSpec((tm, D), lambda i: (i, 0)),
            _resident(g.shape), _resident(wg.shape), _resident(wu.shape), _resident(wd.shape),
        ],
        out_specs=pl.BlockSpec((tm, D), lambda i: (i, 0)),
        out_shape=jax.ShapeDtypeStruct((N, D), F32),
        compiler_params=_params(1),
        name="ffn",
    )(h2, g, wg, wu, wd)


def _conv_kernel(h_ref, g_ref, win_ref, bin_ref, dw_ref, dwb_ref, lng_ref, lnb_ref, wout_ref,
                 o_ref, cbuf_ref, *, tiles_per_seq):
    tm, D = h_ref.shape
    i = pl.program_id(0)
    h = h_ref[...]
    hh = _bdot(_rms_rows(h, g_ref[...]).astype(BF16), win_ref[...]) + bin_ref[...]
    glu = hh[:, :D] * jax.nn.sigmoid(hh[:, D:])

    @pl.when(i % tiles_per_seq == 0)
    def _():
        cbuf_ref[0:CONV_HIST, :] = jnp.zeros((CONV_HIST, D), F32)

    cbuf_ref[CONV_HIST:CONV_HIST + tm, :] = glu
    off = CONV_HIST - (CONV_WIDTH - 1)
    acc = cbuf_ref[off:off + tm, :] * dw_ref[0:1, :]
    for j in range(1, CONV_WIDTH):
        acc = acc + cbuf_ref[off + j:off + j + tm, :] * dw_ref[j:j + 1, :]
    cbuf_ref[0:CONV_HIST, :] = glu[tm - CONV_HIST:tm, :]

    y = acc + dwb_ref[...]
    mu = jnp.mean(y, axis=-1, keepdims=True)
    yc = y - mu
    var = jnp.mean(yc * yc, axis=-1, keepdims=True)
    yn = yc * lax.rsqrt(var + EPS) * lng_ref[...] + lnb_ref[...]
    act = yn * jax.nn.sigmoid(yn)
    o_ref[...] = h + _bdot(act.astype(BF16), wout_ref[...])


def _conv(h2, g, win, bin_, dw, dwb, lng, lnb, wout, *, T):
    N, D = h2.shape
    tm = SEQ_CHUNK
    tps = T // tm
    consts = (g, win, bin_, dw, dwb, lng, lnb, wout)
    return pl.pallas_call(
        functools.partial(_conv_kernel, tiles_per_seq=tps),
        grid=(N // tm,),
        in_specs=[pl.BlockSpec((tm, D), lambda i: (i, 0))] + [_resident(c.shape) for c in consts],
        out_specs=pl.BlockSpec((tm, D), lambda i: (i, 0)),
        out_shape=jax.ShapeDtypeStruct((N, D), F32),
        scratch_shapes=[pltpu.VMEM((CONV_HIST + tm, D), F32)],
        compiler_params=_params(1),
        name="conv",
    )(h2, *consts)


def _rope_tables(T, rot):
    half = rot // 2
    inv = ROPE_THETA ** (-jnp.arange(half, dtype=F32) * 2.0 / rot)
    ang = jnp.arange(T).astype(F32)[:, None] * inv[None, :]
    return jnp.cos(ang).T, jnp.sin(ang).T


def kernel(x, mem, norm_mix, norm_cross, norm_mem, norm_ffn, w_in_ab, a_q_norm, a_k_norm, pool_w, pool_scale, w_out_ab, conv_w_in, conv_b_in, conv_dw_w, conv_dw_b, conv_ln_g, conv_ln_b, conv_w_out, cross_wq, cross_wk, cross_wv, cross_q_norm, cross_k_norm, cross_wo, ffn_w_gate, ffn_w_up, ffn_w_down):
    B, T, D = x.shape
    N = B * T
    depth = norm_mix.shape[0]
    aw = A_HEADS * HEAD_DIM
    iqw = IDX_HEADS * IDX_DIM
    pw = len(POOL_WINDOWS) * POOL_GROUP
    assert T % ROW_TILE == 0 and T % SEQ_CHUNK == 0 and ROW_TILE % SEQ_CHUNK == 0
    row = lambda v: v.reshape(1, -1)

    kT_mem, v_mem = _mem_kv(mem, norm_mem[:, None, :], cross_wk.astype(BF16),
                            cross_wv.astype(BF16), cross_k_norm[:, None, :])
    cos16, sin16 = _rope_tables(T, HEAD_DIM // 4)
    cos8, sin8 = _rope_tables(T, IDX_DIM // 4)

    h = x.reshape(N, D)
    for l in range(depth):
        if l % 2 == 0:
            e = l // 2
            w_in = w_in_ab[e]
            o_u, o_iq, o_iw, o_ik = 3 * aw, 3 * aw + pw, 3 * aw + pw + iqw, 3 * aw + pw + iqw + IDX_HEADS
            wt = jnp.concatenate(
                [w_in[:, :3 * aw], w_in[:, o_iq:o_iw], w_in[:, o_ik:], w_in[:, o_iw:o_ik]],
                axis=1).T.astype(BF16)
            wu = w_in[:, o_u:o_iq].astype(BF16)
            qT, k, vT, iqT, ik, iwT, b2 = _front(
                h, row(norm_mix[l]), wt, wu, a_q_norm[e][:, None], a_k_norm[e][:, None],
                cos16, sin16, cos8, sin8, pool_w[e].astype(BF16), row(pool_scale[e]), B=B, T=T)
            a = _dsa(qT, k, vT, iqT, ik, iwT, B=B, T=T)
            w_out = w_out_ab[e].astype(BF16)
            h = _mixout(h, a.reshape(N, aw), b2, w_out[:aw], w_out[aw:])
        else:
            o = l // 2
            h = _conv(h, row(norm_mix[l]), conv_w_in[o].astype(BF16), row(conv_b_in[o]),
                      conv_dw_w[o], row(conv_dw_b[o]), row(conv_ln_g[o]), row(conv_ln_b[o]),
                      conv_w_out[o].astype(BF16), T=T)
        h = _cross(h, row(norm_cross[l]), cross_wq[l].astype(BF16), row(cross_q_norm[l]),
                   kT_mem[l], v_mem[l], cross_wo[l].astype(BF16), T=T)
        h = _ffn(h, row(norm_ffn[l]), ffn_w_gate[l].astype(BF16), ffn_w_up[l].astype(BF16),
                 ffn_w_down[l].astype(BF16))
    return h.reshape(B, T, D)
```

```python
import functools

import jax
import jax.numpy as jnp
from jax import lax
from jax.experimental import pallas as pl
from jax.experimental.pallas import tpu as pltpu

F32 = jnp.float32
BF16 = jnp.bfloat16

EPS = 1e-6
A_HEADS = 4
HEAD_DIM = 128
IDX_HEADS = 16
IDX_DIM = 64
TOPK_MAX = 256
ROPE_THETA = 500000.0
POOL_WINDOWS = (2, 4, 8, 16)
POOL_GROUP = 128
POOL_HIST = 16
C_HEADS = 4
CONV_WIDTH = 31
CONV_HIST = 32
SUBLANES = 8
LANES = 128

VMEM_LIMIT_BYTES = 56 * 1024 * 1024
SEQ_CHUNK = 256
ROW_TILE = 512
INT32_MIN = -2 ** 31
MASKED_LOGIT = -1e30
V_PAD = 16
V_ROWS = HEAD_DIM + V_PAD
LOG2_E = 1.4426950408889634


def _params(n_axes):
    return pltpu.CompilerParams(
        dimension_semantics=("arbitrary",) * n_axes,
        vmem_limit_bytes=VMEM_LIMIT_BYTES)


def _resident(shape):
    zeros = (0,) * len(shape)
    return pl.BlockSpec(shape, lambda *_: zeros, pipeline_mode=pl.Buffered(1))


def _rms_rows(x, g):
    ms = jnp.mean(x * x, axis=-1, keepdims=True)
    return x * lax.rsqrt(ms + EPS) * g


def _bdot(a, b):
    return jnp.dot(a, b, preferred_element_type=F32)


def _rope_cols(y, cos, sin, half):
    y1, y2 = y[0:half], y[half:2 * half]
    return jnp.concatenate(
        [y1 * cos - y2 * sin, y2 * cos + y1 * sin, y[2 * half:]], axis=0)


def _front_kernel(x_ref, g_ref, wt_ref, wu_ref, gq_ref, gk_ref, cos16_ref, sin16_ref,
                  cos8_ref, sin8_ref, poolw_ref, pscale_ref,
                  qT_ref, k_ref, vT_ref, iqT_ref, ik_ref, iwT_ref, b_ref, ubuf_ref,
                  *, tiles_per_seq):
    tm = x_ref.shape[0]
    i = pl.program_id(0)
    aw = A_HEADS * HEAD_DIM
    iqw = IDX_HEADS * IDX_DIM

    xn = _rms_rows(x_ref[...], g_ref[...])
    xb = xn.astype(BF16)
    xT = xn.T.astype(BF16)
    cols = _bdot(wt_ref[...], xT)

    cos16, sin16 = cos16_ref[...], sin16_ref[...]
    cos8, sin8 = cos8_ref[...], sin8_ref[...]

    def qk_heads(base, gain):
        heads = []
        for h in range(A_HEADS):
            xh = cols[base + h * HEAD_DIM: base + (h + 1) * HEAD_DIM]
            ms = jnp.sum(xh * xh, axis=0, keepdims=True) * (1.0 / HEAD_DIM)
            y = xh * lax.rsqrt(ms + EPS) * gain
            heads.append(_rope_cols(y, cos16, sin16, HEAD_DIM // 8))
        return jnp.concatenate(heads, axis=0)

    qT_ref[0] = qk_heads(0, gq_ref[...]).astype(BF16)
    k_ref[0] = qk_heads(aw, gk_ref[...]).T.astype(BF16)
    ones_blk = (lax.broadcasted_iota(jnp.int32, (V_PAD, tm), 0) == 0).astype(F32)
    v_rows = []
    for h in range(A_HEADS):
        v_rows += [cols[2 * aw + h * HEAD_DIM:2 * aw + (h + 1) * HEAD_DIM], ones_blk]
    vT_ref[0, 0] = jnp.concatenate(v_rows, axis=0).astype(BF16)

    base = 3 * aw
    iq_heads = [
        _rope_cols(cols[base + h * IDX_DIM: base + (h + 1) * IDX_DIM], cos8, sin8, IDX_DIM // 8)
        for h in range(IDX_HEADS)]
    iqT_ref[0] = jnp.concatenate(iq_heads, axis=0).astype(BF16)

    base = 3 * aw + iqw
    ikT = _rope_cols(cols[base:base + IDX_DIM], cos8, sin8, IDX_DIM // 8)
    ikT = jnp.concatenate([ikT, jnp.zeros_like(ikT)], axis=0)
    ik_ref[0] = ikT.T[:, :IDX_DIM].astype(BF16)

    base = 3 * aw + iqw + IDX_DIM
    iwT_ref[0] = cols[base:base + IDX_HEADS] * (IDX_HEADS ** -0.5 * IDX_DIM ** -0.5)

    u = _bdot(xb, wu_ref[...])
    t_in_seq = (i % tiles_per_seq) * tm

    @pl.when(i % tiles_per_seq == 0)
    def _():
        ubuf_ref[0:POOL_HIST, :] = jnp.zeros((POOL_HIST, u.shape[1]), F32)

    ubuf_ref[POOL_HIST:POOL_HIST + tm, :] = u
    t1 = (lax.broadcasted_iota(jnp.int32, (tm, 1), 0) + t_in_seq + 1).astype(F32)
    outs = []
    for g, w in enumerate(POOL_WINDOWS):
        cs = slice(g * POOL_GROUP, (g + 1) * POOL_GROUP)
        acc = u[:, cs]
        for j in range(1, w):
            acc = acc + ubuf_ref[POOL_HIST - j:POOL_HIST - j + tm, cs]
        pooled = acc / jnp.minimum(t1, float(w)) - u[:, cs]
        outs.append(_bdot(pooled.astype(BF16), poolw_ref[g]))
    b_ref[...] = (jnp.concatenate(outs, axis=-1) * pscale_ref[...]).astype(BF16)
    ubuf_ref[0:POOL_HIST, :] = u[tm - POOL_HIST:tm, :]


def _front(x2, g, wt, wu, gq, gk, cos16, sin16, cos8, sin8, poolw, pscale, *, B, T):
    N, D = x2.shape
    tm = SEQ_CHUNK
    tps = T // tm
    aw = A_HEADS * HEAD_DIM
    iqw = IDX_HEADS * IDX_DIM
    pw = len(POOL_WINDOWS) * POOL_GROUP
    seq_map = lambda i: (i // tps, 0, i % tps)
    out_shape = (
        jax.ShapeDtypeStruct((B, aw, T), BF16),
        jax.ShapeDtypeStruct((B, T, aw), BF16),
        jax.ShapeDtypeStruct((B, tps, A_HEADS * V_ROWS, tm), BF16),
        jax.ShapeDtypeStruct((B, iqw, T), BF16),
        jax.ShapeDtypeStruct((B, T, IDX_DIM), BF16),
        jax.ShapeDtypeStruct((B, IDX_HEADS, T), F32),
        jax.ShapeDtypeStruct((N, pw), BF16),
    )
    return pl.pallas_call(
        functools.partial(_front_kernel, tiles_per_seq=tps),
        grid=(N // tm,),
        in_specs=[
            pl.BlockSpec((tm, D), lambda i: (i, 0)),
            _resident(g.shape), _resident(wt.shape), _resident(wu.shape),
            _resident(gq.shape), _resident(gk.shape),
            pl.BlockSpec((HEAD_DIM // 8, tm), lambda i: (0, i % tps)),
            pl.BlockSpec((HEAD_DIM // 8, tm), lambda i: (0, i % tps)),
            pl.BlockSpec((IDX_DIM // 8, tm), lambda i: (0, i % tps)),
            pl.BlockSpec((IDX_DIM // 8, tm), lambda i: (0, i % tps)),
            _resident(poolw.shape), _resident(pscale.shape),
        ],
        out_specs=(
            pl.BlockSpec((1, aw, tm), seq_map),
            pl.BlockSpec((1, tm, aw), lambda i: (i // tps, i % tps, 0)),
            pl.BlockSpec((1, 1, A_HEADS * V_ROWS, tm), lambda i: (i // tps, i % tps, 0, 0)),
            pl.BlockSpec((1, iqw, tm), seq_map),
            pl.BlockSpec((1, tm, IDX_DIM), lambda i: (i // tps, i % tps, 0)),
            pl.BlockSpec((1, IDX_HEADS, tm), seq_map),
            pl.BlockSpec((tm, pw), lambda i: (i, 0)),
        ),
        out_shape=out_shape,
        scratch_shapes=[pltpu.VMEM((POOL_HIST + tm, pw), F32)],
        compiler_params=_params(1),
        name="front",
    )(x2, g, wt, wu, gq, gk, cos16, sin16, cos8, sin8, poolw, pscale)


def _key_to_f32(c):
    bits = c ^ ((c >> 31) & 0x7FFFFFFF)
    return lax.bitcast_convert_type(bits, F32)


def _dsa_kernel(iqT_ref, iwT_ref, qT_ref, ik_ref, k_ref, vT_ref, o_ref,
                s_ref, acc_ref, lg_ref, p_ref, *, topk):
    ch = SEQ_CHUNK
    qi = pl.program_id(1)
    nblk = qi + 1

    row_i = lax.broadcasted_iota(jnp.int32, (ch, ch), 0)
    col_i = lax.broadcasted_iota(jnp.int32, (ch, ch), 1)

    def score_body(j, carry):
        s0 = pl.multiple_of(j * ch, ch)
        lhs = ik_ref[0, pl.ds(s0, ch), :]
        for h in range(IDX_HEADS):
            rhs = iqT_ref[0, h * IDX_DIM:(h + 1) * IDX_DIM, :]
            val = jnp.maximum(_bdot(lhs, rhs), 0.0) * iwT_ref[0, h:h + 1, :]
            if h > 0:
                val = s_ref[pl.ds(s0, ch), :] + val
            if h == IDX_HEADS - 1:
                val = jnp.where(row_i + s0 <= col_i + qi * ch, val, -jnp.inf)
            s_ref[pl.ds(s0, ch), :] = val
        return carry

    lax.fori_loop(0, nblk, score_body, 0)

    def count_ge(cand):
        def body(j, acc):
            s0 = pl.multiple_of(j * ch, ch)
            m = jnp.where(s_ref[pl.ds(s0, ch), :] >= cand, 1.0, 0.0)
            parts = [m[8 * r:8 * r + 8] for r in range(ch // 8)]
            while len(parts) > 1:
                parts = [parts[i] + parts[i + 1] for i in range(0, len(parts), 2)]
            return acc + parts[0]

        acc = lax.fori_loop(0, nblk, body, jnp.zeros((8, ch), F32))
        return jnp.sum(acc, axis=0, keepdims=True)

    kf = float(topk)
    c = jnp.where(count_ge(jnp.zeros((1, ch), F32)) >= kf, 0, INT32_MIN).astype(jnp.int32)

    def bit_body(i, c):
        trial = c + jnp.left_shift(jnp.int32(1), 30 - i)
        return jnp.where(count_ge(_key_to_f32(trial)) >= kf, trial, c)

    c = lax.fori_loop(0, 31, bit_body, c)
    thr = jnp.where(c == INT32_MIN, -3.0e38, _key_to_f32(c))

    log2_scale = HEAD_DIM ** -0.5 * LOG2_E
    acc_ref[...] = jnp.zeros(acc_ref.shape, F32)
    p_ref[...] = jnp.zeros(p_ref.shape, BF16)

    def logits(jn):
        s0 = pl.multiple_of(jn * ch, ch)
        return [_bdot(k_ref[0, pl.ds(s0, ch), h * HEAD_DIM:(h + 1) * HEAD_DIM],
                      qT_ref[0, h * HEAD_DIM:(h + 1) * HEAD_DIM, :]) for h in range(A_HEADS)]

    def stage_logits(lgs):
        for h in range(A_HEADS):
            lg_ref[h * ch:(h + 1) * ch, :] = lgs[h]

    def accumulate_pv(jp, alphas):
        for h in range(A_HEADS):
            vr = slice(h * V_ROWS, (h + 1) * V_ROWS)
            pv = _bdot(vT_ref[0, jp, vr, :], p_ref[h * ch:(h + 1) * ch, :])
            acc_ref[vr, :] = alphas[h] * acc_ref[vr, :] + pv

    stage_logits(logits(0))

    def attn_body(j, carry):
        ms, prev_alphas = carry
        accumulate_pv(jnp.maximum(j - 1, 0), prev_alphas)
        next_lgs = logits(jnp.minimum(j + 1, nblk - 1))
        s0 = pl.multiple_of(j * ch, ch)
        sel = s_ref[pl.ds(s0, ch), :] >= thr
        new_ms, alphas = [], []
        for h in range(A_HEADS):
            lg = jnp.where(sel, lg_ref[h * ch:(h + 1) * ch, :] * log2_scale, MASKED_LOGIT)
            m_new = jnp.maximum(ms[h], jnp.max(lg, axis=0, keepdims=True))
            alphas.append(jnp.exp2(ms[h] - m_new))
            p_ref[h * ch:(h + 1) * ch, :] = jnp.exp2(lg - m_new).astype(BF16)
            new_ms.append(m_new)
        stage_logits(next_lgs)
        return tuple(new_ms), tuple(alphas)

    init = (tuple(jnp.full((1, ch), MASKED_LOGIT, F32) for _ in range(A_HEADS)),
            tuple(jnp.ones((1, ch), F32) for _ in range(A_HEADS)))
    _, last_alphas = lax.fori_loop(0, nblk, attn_body, init)
    accumulate_pv(nblk - 1, last_alphas)
    outs = []
    for h in range(A_HEADS):
        denom = acc_ref[h * V_ROWS + HEAD_DIM:h * V_ROWS + HEAD_DIM + 1, :]
        outs.append(acc_ref[h * V_ROWS:h * V_ROWS + HEAD_DIM, :] * (1.0 / denom))
    o_ref[0] = jnp.concatenate(outs, axis=0).T.astype(BF16)


def _dsa(qT, k, vT, iqT, ik, iwT, *, B, T):
    ch = SEQ_CHUNK
    aw = A_HEADS * HEAD_DIM
    iqw = IDX_HEADS * IDX_DIM
    topk = min(TOPK_MAX, T // 4)
    blk = lambda b, q: (b, 0, q)
    per_batch = lambda shape: pl.BlockSpec(
        (1,) + shape, lambda b, q: (b,) + (0,) * len(shape), pipeline_mode=pl.Buffered(1))
    return pl.pallas_call(
        functools.partial(_dsa_kernel, topk=topk),
        grid=(B, T // ch),
        in_specs=[
            pl.BlockSpec((1, iqw, ch), blk),
            pl.BlockSpec((1, IDX_HEADS, ch), blk),
            pl.BlockSpec((1, aw, ch), blk),
            per_batch((T, IDX_DIM)),
            per_batch((T, aw)),
            per_batch((T // ch, A_HEADS * V_ROWS, ch)),
        ],
        out_specs=pl.BlockSpec((1, ch, aw), lambda b, q: (b, q, 0)),
        out_shape=jax.ShapeDtypeStruct((B, T, aw), BF16),
        scratch_shapes=[
            pltpu.VMEM((T, ch), F32),
            pltpu.VMEM((A_HEADS * V_ROWS, ch), F32),
            pltpu.VMEM((A_HEADS * ch, ch), F32),
            pltpu.VMEM((A_HEADS * ch, ch), BF16),
        ],
        compiler_params=_params(2),
        name="dsa",
    )(iqT, iwT, qT, ik, k, vT)


def _mixout_kernel(x_ref, a_ref, b_ref, wa_ref, wb_ref, o_ref):
    o_ref[...] = x_ref[...] + _bdot(a_ref[...], wa_ref[...]) + _bdot(b_ref[...], wb_ref[...])


def _mixout(x2, a2, b2, wa, wb):
    N, D = x2.shape
    tm = ROW_TILE
    row = lambda w: pl.BlockSpec((tm, w), lambda i: (i, 0))
    return pl.pallas_call(
        _mixout_kernel,
        grid=(N // tm,),
        in_specs=[row(D), row(a2.shape[1]), row(b2.shape[1]), _resident(wa.shape), _resident(wb.shape)],
        out_specs=row(D),
        out_shape=jax.ShapeDtypeStruct((N, D), F32),
        compiler_params=_params(1),
        name="mixout",
    )(x2, a2, b2, wa, wb)


def _mem_kernel(mem_ref, g_ref, wk_ref, wv_ref, gk_ref, kT_ref, v_ref):
    dh = gk_ref.shape[-1]
    mn = _rms_rows(mem_ref[0], g_ref[0]).astype(BF16)
    k = _bdot(mn, wk_ref[0])
    v_ref[0, 0] = _bdot(mn, wv_ref[0]).astype(BF16)
    for c in range(C_HEADS):
        kc = _rms_rows(k[:, c * dh:(c + 1) * dh], gk_ref[0])
        kT_ref[0, 0, c] = kc.T.astype(BF16)


def _mem_kv(mem, g, wk, wv, gk):
    B, M, D = mem.shape
    L = wk.shape[0]
    dh = D // C_HEADS
    lay = lambda *tail: pl.BlockSpec((1,) + tail, lambda l, b: (l,) + (0,) * len(tail))
    return pl.pallas_call(
        _mem_kernel,
        grid=(L, B),
        in_specs=[
            pl.BlockSpec((1, M, D), lambda l, b: (b, 0, 0)),
            lay(1, D), lay(D, D), lay(D, D), lay(1, dh),
        ],
        out_specs=(
            pl.BlockSpec((1, 1, C_HEADS, dh, M), lambda l, b: (l, b, 0, 0, 0)),
            pl.BlockSpec((1, 1, M, D), lambda l, b: (l, b, 0, 0)),
        ),
        out_shape=(
            jax.ShapeDtypeStruct((L, B, C_HEADS, dh, M), BF16),
            jax.ShapeDtypeStruct((L, B, M, D), BF16),
        ),
        compiler_params=_params(2),
        name="mem_kv",
    )(mem, g, wk, wv, gk)


def _cross_kernel(h_ref, g_ref, wq_ref, gq_ref, kT_ref, v_ref, wo_ref, o_ref):
    dh = gq_ref.shape[-1]
    h = h_ref[...]
    q = _bdot(_rms_rows(h, g_ref[...]).astype(BF16), wq_ref[...])
    heads = []
    for c in range(C_HEADS):
        cs = slice(c * dh, (c + 1) * dh)
        qc = _rms_rows(q[:, cs], gq_ref[...]).astype(BF16)
        lg = _bdot(qc, kT_ref[0, c]) * dh ** -0.5
        p = jnp.exp(lg - jnp.max(lg, axis=-1, keepdims=True))
        p = p * (1.0 / jnp.sum(p, axis=-1, keepdims=True))
        heads.append(_bdot(p.astype(BF16), v_ref[0, :, cs]))
    o = jnp.concatenate(heads, axis=-1).astype(BF16)
    o_ref[...] = h + _bdot(o, wo_ref[...])


def _cross(h2, g, wq, gq, kT, v, wo, *, T):
    N, D = h2.shape
    tm = ROW_TILE
    tps = T // tm
    dh = D // C_HEADS
    M = v.shape[1]
    return pl.pallas_call(
        _cross_kernel,
        grid=(N // tm,),
        in_specs=[
            pl.BlockSpec((tm, D), lambda i: (i, 0)),
            _resident(g.shape), _resident(wq.shape), _resident(gq.shape),
            pl.BlockSpec((1, C_HEADS, dh, M), lambda i: (i // tps, 0, 0, 0)),
            pl.BlockSpec((1, M, D), lambda i: (i // tps, 0, 0)),
            _resident(wo.shape),
        ],
        out_specs=pl.BlockSpec((tm, D), lambda i: (i, 0)),
        out_shape=jax.ShapeDtypeStruct((N, D), F32),
        compiler_params=_params(1),
        name="cross",
    )(h2, g, wq, gq, kT, v, wo)


def _ffn_kernel(h_ref, g_ref, wg_ref, wu_ref, wd_ref, o_ref, *, n_chunks):
    h = h_ref[...]
    hn = _rms_rows(h, g_ref[...]).astype(BF16)
    fc = wg_ref.shape[1] // n_chunks
    acc = h
    for c in range(n_chunks):
        cs = slice(c * fc, (c + 1) * fc)
        gate = _bdot(hn, wg_ref[:, cs])
        act = gate * jax.nn.sigmoid(gate) * _bdot(hn, wu_ref[:, cs])
        acc = acc + _bdot(act.astype(BF16), wd_ref[cs, :])
    o_ref[...] = acc


def _ffn(h2, g, wg, wu, wd):
    N, D = h2.shape
    tm = ROW_TILE
    return pl.pallas_call(
        functools.partial(_ffn_kernel, n_chunks=2),
        grid=(N // tm,),
        in_specs=[
            pl.BlockSpec((tm, D), lambda i: (i, 0)),
            _resident(g.shape), _resident(wg.shape), _resident(wu.shape), _resident(wd.shape),
        ],
        out_specs=pl.BlockSpec((tm, D), lambda i: (i, 0)),
        out_shape=jax.ShapeDtypeStruct((N, D), F32),
        compiler_params=_params(1),
        name="ffn",
    )(h2, g, wg, wu, wd)


def _conv_kernel(h_ref, g_ref, win_ref, bin_ref, dw_ref, dwb_ref, lng_ref, lnb_ref, wout_ref,
                 o_ref, cbuf_ref, shift_ref, *, tiles_per_seq):
    tm, D = h_ref.shape
    i = pl.program_id(0)
    h = h_ref[...]
    hh = _bdot(_rms_rows(h, g_ref[...]).astype(BF16), win_ref[...]) + bin_ref[...]
    glu = hh[:, :D] * jax.nn.sigmoid(hh[:, D:])

    @pl.when(i % tiles_per_seq == 0)
    def _():
        cbuf_ref[0:CONV_HIST, :] = jnp.zeros((CONV_HIST, D), F32)

    cbuf_ref[CONV_HIST:CONV_HIST + tm, :] = glu
    off = CONV_HIST - (CONV_WIDTH - 1)
    span = tm + CONV_HIST - SUBLANES
    for r in range(1, SUBLANES):
        shift_ref[r - 1] = cbuf_ref[r:r + span, :]
    blocks = []
    for cb in range(D // LANES):
        cs = slice(cb * LANES, (cb + 1) * LANES)
        acc = None
        for j in range(CONV_WIDTH):
            a, r = divmod(off + j, SUBLANES)
            src = cbuf_ref if r == 0 else shift_ref.at[r - 1]
            term = src[a * SUBLANES:a * SUBLANES + tm, cs] * dw_ref[j:j + 1, cs]
            acc = term if acc is None else acc + term
        blocks.append(acc)
    cbuf_ref[0:CONV_HIST, :] = glu[tm - CONV_HIST:tm, :]

    y = jnp.concatenate(blocks, axis=-1) + dwb_ref[...]
    mu = jnp.mean(y, axis=-1, keepdims=True)
    yc = y - mu
    var = jnp.mean(yc * yc, axis=-1, keepdims=True)
    yn = yc * lax.rsqrt(var + EPS) * lng_ref[...] + lnb_ref[...]
    act = yn * jax.nn.sigmoid(yn)
    o_ref[...] = h + _bdot(act.astype(BF16), wout_ref[...])


def _conv(h2, g, win, bin_, dw, dwb, lng, lnb, wout, *, T):
    N, D = h2.shape
    tm = SEQ_CHUNK
    tps = T // tm
    consts = (g, win, bin_, dw, dwb, lng, lnb, wout)
    return pl.pallas_call(
        functools.partial(_conv_kernel, tiles_per_seq=tps),
        grid=(N // tm,),
        in_specs=[pl.BlockSpec((tm, D), lambda i: (i, 0))] + [_resident(c.shape) for c in consts],
        out_specs=pl.BlockSpec((tm, D), lambda i: (i, 0)),
        out_shape=jax.ShapeDtypeStruct((N, D), F32),
        scratch_shapes=[
            pltpu.VMEM((CONV_HIST + tm, D), F32),
            pltpu.VMEM((SUBLANES - 1, tm + CONV_HIST - SUBLANES, D), F32),
        ],
        compiler_params=_params(1),
        name="conv",
    )(h2, *consts)


def _rope_tables(T, rot):
    half = rot // 2
    inv = ROPE_THETA ** (-jnp.arange(half, dtype=F32) * 2.0 / rot)
    ang = jnp.arange(T).astype(F32)[:, None] * inv[None, :]
    return jnp.cos(ang).T, jnp.sin(ang).T


def kernel(x, mem, norm_mix, norm_cross, norm_mem, norm_ffn, w_in_ab, a_q_norm, a_k_norm, pool_w, pool_scale, w_out_ab, conv_w_in, conv_b_in, conv_dw_w, conv_dw_b, conv_ln_g, conv_ln_b, conv_w_out, cross_wq, cross_wk, cross_wv, cross_q_norm, cross_k_norm, cross_wo, ffn_w_gate, ffn_w_up, ffn_w_down):
    B, T, D = x.shape
    N = B * T
    depth = norm_mix.shape[0]
    aw = A_HEADS * HEAD_DIM
    iqw = IDX_HEADS * IDX_DIM
    pw = len(POOL_WINDOWS) * POOL_GROUP
    assert T % ROW_TILE == 0 and T % SEQ_CHUNK == 0 and ROW_TILE % SEQ_CHUNK == 0
    row = lambda v: v.reshape(1, -1)

    kT_mem, v_mem = _mem_kv(mem, norm_mem[:, None, :], cross_wk.astype(BF16),
                            cross_wv.astype(BF16), cross_k_norm[:, None, :])
    cos16, sin16 = _rope_tables(T, HEAD_DIM // 4)
    cos8, sin8 = _rope_tables(T, IDX_DIM // 4)

    h = x.reshape(N, D)
    for l in range(depth):
        if l % 2 == 0:
            e = l // 2
            w_in = w_in_ab[e]
            o_u, o_iq, o_iw, o_ik = 3 * aw, 3 * aw + pw, 3 * aw + pw + iqw, 3 * aw + pw + iqw + IDX_HEADS
            wt = jnp.concatenate(
                [w_in[:, :3 * aw], w_in[:, o_iq:o_iw], w_in[:, o_ik:], w_in[:, o_iw:o_ik]],
                axis=1).T.astype(BF16)
            wu = w_in[:, o_u:o_iq].astype(BF16)
            qT, k, vT, iqT, ik, iwT, b2 = _front(
                h, row(norm_mix[l]), wt, wu, a_q_norm[e][:, None], a_k_norm[e][:, None],
                cos16, sin16, cos8, sin8, pool_w[e].astype(BF16), row(pool_scale[e]), B=B, T=T)
            a = _dsa(qT, k, vT, iqT, ik, iwT, B=B, T=T)
            w_out = w_out_ab[e].astype(BF16)
            h = _mixout(h, a.reshape(N, aw), b2, w_out[:aw], w_out[aw:])
        else:
            o = l // 2
            h = _conv(h, row(norm_mix[l]), conv_w_in[o].astype(BF16), row(conv_b_in[o]),
                      conv_dw_w[o], row(conv_dw_b[o]), row(conv_ln_g[o]), row(conv_ln_b[o]),
                      conv_w_out[o].astype(BF16), T=T)
        h = _cross(h, row(norm_cross[l]), cross_wq[l].astype(BF16), row(cross_q_norm[l]),
                   kT_mem[l], v_mem[l], cross_wo[l].astype(BF16), T=T)
        h = _ffn(h, row(norm_ffn[l]), ffn_w_gate[l].astype(BF16), ffn_w_up[l].astype(BF16),
                 ffn_w_down[l].astype(BF16))
    return h.reshape(B, T, D)
```

```python
import functools

import jax
import jax.numpy as jnp
from jax import lax
from jax.experimental import pallas as pl
from jax.experimental.pallas import tpu as pltpu

F32 = jnp.float32
BF16 = jnp.bfloat16

EPS = 1e-6
A_HEADS = 4
HEAD_DIM = 128
IDX_HEADS = 16
IDX_DIM = 64
TOPK_MAX = 256
ROPE_THETA = 500000.0
POOL_WINDOWS = (2, 4, 8, 16)
POOL_GROUP = 128
POOL_HIST = 16
C_HEADS = 4
CONV_WIDTH = 31
CONV_HIST = 32
SUBLANES = 8
LANES = 128
BF16_ROWS = 16

VMEM_LIMIT_BYTES = 56 * 1024 * 1024
QUERY_BLOCK = 256
KEY_CHUNK = 512
CONV_TILE = 256
ROW_TILE = 512
INT32_MIN = -2 ** 31
MASKED_LOGIT = -1e30
V_PAD = 16
V_ROWS = HEAD_DIM + V_PAD
LOG2_E = 1.4426950408889634


def _params(n_axes):
    return pltpu.CompilerParams(
        dimension_semantics=("arbitrary",) * n_axes,
        vmem_limit_bytes=VMEM_LIMIT_BYTES)


def _resident(shape):
    zeros = (0,) * len(shape)
    return pl.BlockSpec(shape, lambda *_: zeros, pipeline_mode=pl.Buffered(1))


def _rms_rows(x, g):
    ms = jnp.mean(x * x, axis=-1, keepdims=True)
    return x * lax.rsqrt(ms + EPS) * g


def _bdot(a, b):
    return jnp.dot(a, b, preferred_element_type=F32)


def _rope_cols(y, cos, sin, half):
    y1, y2 = y[0:half], y[half:2 * half]
    return jnp.concatenate(
        [y1 * cos - y2 * sin, y2 * cos + y1 * sin, y[2 * half:]], axis=0)


def _front_kernel(x_ref, g_ref, wt_ref, wu_ref, gq_ref, gk_ref, cos16_ref, sin16_ref,
                  cos8_ref, sin8_ref, poolw_ref, pscale_ref,
                  qT_ref, k_ref, vT_ref, iqT_ref, ik_ref, iwT_ref, b_ref, ubuf_ref,
                  *, tiles_per_seq):
    tm = x_ref.shape[0]
    i = pl.program_id(0)
    aw = A_HEADS * HEAD_DIM
    iqw = IDX_HEADS * IDX_DIM

    xn = _rms_rows(x_ref[...], g_ref[...])
    xb = xn.astype(BF16)
    xT = xn.T.astype(BF16)
    cols = _bdot(wt_ref[...], xT)

    cos16, sin16 = cos16_ref[...], sin16_ref[...]
    cos8, sin8 = cos8_ref[...], sin8_ref[...]

    def qk_heads(base, gain):
        heads = []
        for h in range(A_HEADS):
            xh = cols[base + h * HEAD_DIM: base + (h + 1) * HEAD_DIM]
            ms = jnp.sum(xh * xh, axis=0, keepdims=True) * (1.0 / HEAD_DIM)
            y = xh * lax.rsqrt(ms + EPS) * gain
            heads.append(_rope_cols(y, cos16, sin16, HEAD_DIM // 8))
        return jnp.concatenate(heads, axis=0)

    qT_ref[0] = qk_heads(0, gq_ref[...]).astype(BF16)
    k_ref[0] = qk_heads(aw, gk_ref[...]).T.astype(BF16)
    ones_blk = (lax.broadcasted_iota(jnp.int32, (V_PAD, tm), 0) == 0).astype(F32)
    v_rows = []
    for h in range(A_HEADS):
        v_rows += [cols[2 * aw + h * HEAD_DIM:2 * aw + (h + 1) * HEAD_DIM], ones_blk]
    vT_ref[0, 0] = jnp.concatenate(v_rows, axis=0).astype(BF16)

    base = 3 * aw
    iq_heads = [
        _rope_cols(cols[base + h * IDX_DIM: base + (h + 1) * IDX_DIM], cos8, sin8, IDX_DIM // 8)
        for h in range(IDX_HEADS)]
    iqT_ref[0] = jnp.concatenate(iq_heads, axis=0).astype(BF16)

    base = 3 * aw + iqw
    ikT = _rope_cols(cols[base:base + IDX_DIM], cos8, sin8, IDX_DIM // 8)
    ikT = jnp.concatenate([ikT, jnp.zeros_like(ikT)], axis=0)
    ik_ref[0] = ikT.T[:, :IDX_DIM].astype(BF16)

    base = 3 * aw + iqw + IDX_DIM
    iwT_ref[0] = cols[base:base + IDX_HEADS] * (IDX_HEADS ** -0.5 * IDX_DIM ** -0.5)

    u = _bdot(xb, wu_ref[...])
    t_in_seq = (i % tiles_per_seq) * tm

    @pl.when(i % tiles_per_seq == 0)
    def _():
        ubuf_ref[0:POOL_HIST, :] = jnp.zeros((POOL_HIST, u.shape[1]), F32)

    ubuf_ref[POOL_HIST:POOL_HIST + tm, :] = u
    t1 = (lax.broadcasted_iota(jnp.int32, (tm, 1), 0) + t_in_seq + 1).astype(F32)
    outs = []
    for g, w in enumerate(POOL_WINDOWS):
        cs = slice(g * POOL_GROUP, (g + 1) * POOL_GROUP)
        acc = u[:, cs]
        for j in range(1, w):
            acc = acc + ubuf_ref[POOL_HIST - j:POOL_HIST - j + tm, cs]
        pooled = acc / jnp.minimum(t1, float(w)) - u[:, cs]
        outs.append(_bdot(pooled.astype(BF16), poolw_ref[g]))
    b_ref[...] = (jnp.concatenate(outs, axis=-1) * pscale_ref[...]).astype(BF16)
    ubuf_ref[0:POOL_HIST, :] = u[tm - POOL_HIST:tm, :]


def _front(x2, g, wt, wu, gq, gk, cos16, sin16, cos8, sin8, poolw, pscale, *, B, T):
    N, D = x2.shape
    tm = KEY_CHUNK
    tps = T // tm
    aw = A_HEADS * HEAD_DIM
    iqw = IDX_HEADS * IDX_DIM
    pw = len(POOL_WINDOWS) * POOL_GROUP
    seq_map = lambda i: (i // tps, 0, i % tps)
    out_shape = (
        jax.ShapeDtypeStruct((B, aw, T), BF16),
        jax.ShapeDtypeStruct((B, T, aw), BF16),
        jax.ShapeDtypeStruct((B, tps, A_HEADS * V_ROWS, tm), BF16),
        jax.ShapeDtypeStruct((B, iqw, T), BF16),
        jax.ShapeDtypeStruct((B, T, IDX_DIM), BF16),
        jax.ShapeDtypeStruct((B, IDX_HEADS, T), F32),
        jax.ShapeDtypeStruct((N, pw), BF16),
    )
    return pl.pallas_call(
        functools.partial(_front_kernel, tiles_per_seq=tps),
        grid=(N // tm,),
        in_specs=[
            pl.BlockSpec((tm, D), lambda i: (i, 0)),
            _resident(g.shape), _resident(wt.shape), _resident(wu.shape),
            _resident(gq.shape), _resident(gk.shape),
            pl.BlockSpec((HEAD_DIM // 8, tm), lambda i: (0, i % tps)),
            pl.BlockSpec((HEAD_DIM // 8, tm), lambda i: (0, i % tps)),
            pl.BlockSpec((IDX_DIM // 8, tm), lambda i: (0, i % tps)),
            pl.BlockSpec((IDX_DIM // 8, tm), lambda i: (0, i % tps)),
            _resident(poolw.shape), _resident(pscale.shape),
        ],
        out_specs=(
            pl.BlockSpec((1, aw, tm), seq_map),
            pl.BlockSpec((1, tm, aw), lambda i: (i // tps, i % tps, 0)),
            pl.BlockSpec((1, 1, A_HEADS * V_ROWS, tm), lambda i: (i // tps, i % tps, 0, 0)),
            pl.BlockSpec((1, iqw, tm), seq_map),
            pl.BlockSpec((1, tm, IDX_DIM), lambda i: (i // tps, i % tps, 0)),
            pl.BlockSpec((1, IDX_HEADS, tm), seq_map),
            pl.BlockSpec((tm, pw), lambda i: (i, 0)),
        ),
        out_shape=out_shape,
        scratch_shapes=[pltpu.VMEM((POOL_HIST + tm, pw), F32)],
        compiler_params=_params(1),
        name="front",
    )(x2, g, wt, wu, gq, gk, cos16, sin16, cos8, sin8, poolw, pscale)


def _key_to_f32(c):
    bits = c ^ ((c >> 31) & 0x7FFFFFFF)
    return lax.bitcast_convert_type(bits, F32)


def _high_half(x):
    bits = lax.bitcast_convert_type(x, jnp.int32) & jnp.int32(-65536)
    return lax.bitcast_convert_type(bits, F32).astype(BF16)


def _dsa_kernel(iqT_ref, iwT_ref, qT_ref, ik_ref, k_ref, vT_ref, o_ref,
                s_ref, s16_ref, acc_ref, lg_ref, p_ref, *, topk):
    qb, kc = QUERY_BLOCK, KEY_CHUNK
    qi = pl.program_id(1)
    nblk = ((qi + 1) * qb + kc - 1) // kc

    row_i = lax.broadcasted_iota(jnp.int32, (kc, qb), 0)
    col_i = lax.broadcasted_iota(jnp.int32, (kc, qb), 1)

    def score_body(j, carry):
        s0 = pl.multiple_of(j * kc, kc)
        lhs = ik_ref[0, pl.ds(s0, kc), :]
        for h in range(IDX_HEADS):
            rhs = iqT_ref[0, h * IDX_DIM:(h + 1) * IDX_DIM, :]
            val = jnp.maximum(_bdot(lhs, rhs), 0.0) * iwT_ref[0, h:h + 1, :]
            if h > 0:
                val = s_ref[pl.ds(s0, kc), :] + val
            if h == IDX_HEADS - 1:
                val = jnp.where(row_i + s0 <= col_i + qi * qb, val, -jnp.inf)
                s16_ref[pl.ds(s0, kc), :] = _high_half(val)
            s_ref[pl.ds(s0, kc), :] = val
        return carry

    lax.fori_loop(0, nblk, score_body, 0)

    def count_ge_high(cand16):
        def body(j, acc):
            s0 = pl.multiple_of(j * kc, kc)
            blk = s16_ref[pl.ds(s0, kc), :]
            one = jnp.ones((), BF16)
            parts = [jnp.zeros((BF16_ROWS, qb), BF16) for _ in range(4)]
            for r in range(kc // BF16_ROWS):
                a = parts[r % 4]
                parts[r % 4] = jnp.where(blk[BF16_ROWS * r:BF16_ROWS * (r + 1)] >= cand16, a + one, a)
            part = (parts[0] + parts[1]) + (parts[2] + parts[3])
            return acc + part.astype(F32)

        acc = lax.fori_loop(0, nblk, body, jnp.zeros((BF16_ROWS, qb), F32))
        return jnp.sum(acc, axis=0, keepdims=True)

    def count_ge(cand):
        def body(j, accs):
            s0 = pl.multiple_of(j * kc, kc)
            blk = s_ref[pl.ds(s0, kc), :]
            accs = list(accs)
            for r in range(kc // SUBLANES):
                a = accs[r % len(accs)]
                accs[r % len(accs)] = jnp.where(blk[SUBLANES * r:SUBLANES * (r + 1)] >= cand, a + 1.0, a)
            return tuple(accs)

        accs = lax.fori_loop(0, nblk, body, tuple(jnp.zeros((SUBLANES, qb), F32) for _ in range(4)))
        return jnp.sum(accs[0] + accs[1] + accs[2] + accs[3], axis=0, keepdims=True)

    kf = float(topk)
    zero16 = jnp.zeros((1, qb), BF16)
    c = jnp.where(count_ge_high(zero16) >= kf, 0, INT32_MIN).astype(jnp.int32)

    def high_bit_body(i, c):
        trial = c + jnp.left_shift(jnp.int32(1), 30 - i)
        return jnp.where(count_ge_high(_high_half(_key_to_f32(trial))) >= kf, trial, c)

    def low_bit_body(i, c):
        trial = c + jnp.left_shift(jnp.int32(1), 15 - i)
        return jnp.where(count_ge(_key_to_f32(trial)) >= kf, trial, c)

    c = lax.fori_loop(0, 15, high_bit_body, c)
    c = lax.fori_loop(0, 16, low_bit_body, c)
    thr = jnp.where(c == INT32_MIN, -3.0e38, _key_to_f32(c))

    log2_scale = HEAD_DIM ** -0.5 * LOG2_E
    acc_ref[...] = jnp.zeros(acc_ref.shape, F32)
    p_ref[...] = jnp.zeros(p_ref.shape, BF16)

    def logits(jn):
        s0 = pl.multiple_of(jn * kc, kc)
        return [_bdot(k_ref[0, pl.ds(s0, kc), h * HEAD_DIM:(h + 1) * HEAD_DIM],
                      qT_ref[0, h * HEAD_DIM:(h + 1) * HEAD_DIM, :]) for h in range(A_HEADS)]

    def stage_logits(lgs):
        for h in range(A_HEADS):
            lg_ref[h * kc:(h + 1) * kc, :] = lgs[h]

    def accumulate_pv(jp, alphas):
        for h in range(A_HEADS):
            vr = slice(h * V_ROWS, (h + 1) * V_ROWS)
            pv = _bdot(vT_ref[0, jp, vr, :], p_ref[h * kc:(h + 1) * kc, :])
            acc_ref[vr, :] = alphas[h] * acc_ref[vr, :] + pv

    stage_logits(logits(0))

    def attn_body(j, carry):
        ms, prev_alphas = carry
        accumulate_pv(jnp.maximum(j - 1, 0), prev_alphas)
        next_lgs = logits(jnp.minimum(j + 1, nblk - 1))
        s0 = pl.multiple_of(j * kc, kc)
        sel = s_ref[pl.ds(s0, kc), :] >= thr
        new_ms, alphas = [], []
        for h in range(A_HEADS):
            lg = jnp.where(sel, lg_ref[h * kc:(h + 1) * kc, :] * log2_scale, MASKED_LOGIT)
            m_new = jnp.maximum(ms[h], jnp.max(lg, axis=0, keepdims=True))
            alphas.append(jnp.exp2(ms[h] - m_new))
            p_ref[h * kc:(h + 1) * kc, :] = jnp.exp2(lg - m_new).astype(BF16)
            new_ms.append(m_new)
        stage_logits(next_lgs)
        return tuple(new_ms), tuple(alphas)

    init = (tuple(jnp.full((1, qb), MASKED_LOGIT, F32) for _ in range(A_HEADS)),
            tuple(jnp.ones((1, qb), F32) for _ in range(A_HEADS)))
    _, last_alphas = lax.fori_loop(0, nblk, attn_body, init)
    accumulate_pv(nblk - 1, last_alphas)
    outs = []
    for h in range(A_HEADS):
        denom = acc_ref[h * V_ROWS + HEAD_DIM:h * V_ROWS + HEAD_DIM + 1, :]
        outs.append(acc_ref[h * V_ROWS:h * V_ROWS + HEAD_DIM, :] * (1.0 / denom))
    o_ref[0] = jnp.concatenate(outs, axis=0).T.astype(BF16)


def _dsa(qT, k, vT, iqT, ik, iwT, *, B, T):
    qb, kc = QUERY_BLOCK, KEY_CHUNK
    aw = A_HEADS * HEAD_DIM
    iqw = IDX_HEADS * IDX_DIM
    topk = min(TOPK_MAX, T // 4)
    blk = lambda b, q: (b, 0, q)
    per_batch = lambda shape: pl.BlockSpec(
        (1,) + shape, lambda b, q: (b,) + (0,) * len(shape), pipeline_mode=pl.Buffered(1))
    return pl.pallas_call(
        functools.partial(_dsa_kernel, topk=topk),
        grid=(B, T // qb),
        in_specs=[
            pl.BlockSpec((1, iqw, qb), blk),
            pl.BlockSpec((1, IDX_HEADS, qb), blk),
            pl.BlockSpec((1, aw, qb), blk),
            per_batch((T, IDX_DIM)),
            per_batch((T, aw)),
            per_batch((T // kc, A_HEADS * V_ROWS, kc)),
        ],
        out_specs=pl.BlockSpec((1, qb, aw), lambda b, q: (b, q, 0)),
        out_shape=jax.ShapeDtypeStruct((B, T, aw), BF16),
        scratch_shapes=[
            pltpu.VMEM((T, qb), F32),
            pltpu.VMEM((T, qb), BF16),
            pltpu.VMEM((A_HEADS * V_ROWS, qb), F32),
            pltpu.VMEM((A_HEADS * kc, qb), F32),
            pltpu.VMEM((A_HEADS * kc, qb), BF16),
        ],
        compiler_params=_params(2),
        name="dsa",
    )(iqT, iwT, qT, ik, k, vT)


def _mixout_kernel(x_ref, a_ref, b_ref, wa_ref, wb_ref, o_ref):
    o_ref[...] = x_ref[...] + _bdot(a_ref[...], wa_ref[...]) + _bdot(b_ref[...], wb_ref[...])


def _mixout(x2, a2, b2, wa, wb):
    N, D = x2.shape
    tm = ROW_TILE
    row = lambda w: pl.BlockSpec((tm, w), lambda i: (i, 0))
    return pl.pallas_call(
        _mixout_kernel,
        grid=(N // tm,),
        in_specs=[row(D), row(a2.shape[1]), row(b2.shape[1]), _resident(wa.shape), _resident(wb.shape)],
        out_specs=row(D),
        out_shape=jax.ShapeDtypeStruct((N, D), F32),
        compiler_params=_params(1),
        name="mixout",
    )(x2, a2, b2, wa, wb)


def _mem_kernel(mem_ref, g_ref, wk_ref, wv_ref, gk_ref, kT_ref, v_ref):
    dh = gk_ref.shape[-1]
    mn = _rms_rows(mem_ref[0], g_ref[0]).astype(BF16)
    k = _bdot(mn, wk_ref[0])
    v_ref[0, 0] = _bdot(mn, wv_ref[0]).astype(BF16)
    for c in range(C_HEADS):
        kc = _rms_rows(k[:, c * dh:(c + 1) * dh], gk_ref[0])
        kT_ref[0, 0, c] = kc.T.astype(BF16)


def _mem_kv(mem, g, wk, wv, gk):
    B, M, D = mem.shape
    L = wk.shape[0]
    dh = D // C_HEADS
    lay = lambda *tail: pl.BlockSpec((1,) + tail, lambda l, b: (l,) + (0,) * len(tail))
    return pl.pallas_call(
        _mem_kernel,
        grid=(L, B),
        in_specs=[
            pl.BlockSpec((1, M, D), lambda l, b: (b, 0, 0)),
            lay(1, D), lay(D, D), lay(D, D), lay(1, dh),
        ],
        out_specs=(
            pl.BlockSpec((1, 1, C_HEADS, dh, M), lambda l, b: (l, b, 0, 0, 0)),
            pl.BlockSpec((1, 1, M, D), lambda l, b: (l, b, 0, 0)),
        ),
        out_shape=(
            jax.ShapeDtypeStruct((L, B, C_HEADS, dh, M), BF16),
            jax.ShapeDtypeStruct((L, B, M, D), BF16),
        ),
        compiler_params=_params(2),
        name="mem_kv",
    )(mem, g, wk, wv, gk)


def _cross_kernel(h_ref, g_ref, wq_ref, gq_ref, kT_ref, v_ref, wo_ref, o_ref):
    dh = gq_ref.shape[-1]
    h = h_ref[...]
    q = _bdot(_rms_rows(h, g_ref[...]).astype(BF16), wq_ref[...])
    heads = []
    for c in range(C_HEADS):
        cs = slice(c * dh, (c + 1) * dh)
        qc = _rms_rows(q[:, cs], gq_ref[...]).astype(BF16)
        lg = _bdot(qc, kT_ref[0, c]) * dh ** -0.5
        p = jnp.exp(lg - jnp.max(lg, axis=-1, keepdims=True))
        p = p * (1.0 / jnp.sum(p, axis=-1, keepdims=True))
        heads.append(_bdot(p.astype(BF16), v_ref[0, :, cs]))
    o = jnp.concatenate(heads, axis=-1).astype(BF16)
    o_ref[...] = h + _bdot(o, wo_ref[...])


def _cross(h2, g, wq, gq, kT, v, wo, *, T):
    N, D = h2.shape
    tm = ROW_TILE
    tps = T // tm
    dh = D // C_HEADS
    M = v.shape[1]
    return pl.pallas_call(
        _cross_kernel,
        grid=(N // tm,),
        in_specs=[
            pl.BlockSpec((tm, D), lambda i: (i, 0)),
            _resident(g.shape), _resident(wq.shape), _resident(gq.shape),
            pl.BlockSpec((1, C_HEADS, dh, M), lambda i: (i // tps, 0, 0, 0)),
            pl.BlockSpec((1, M, D), lambda i: (i // tps, 0, 0)),
            _resident(wo.shape),
        ],
        out_specs=pl.BlockSpec((tm, D), lambda i: (i, 0)),
        out_shape=jax.ShapeDtypeStruct((N, D), F32),
        compiler_params=_params(1),
        name="cross",
    )(h2, g, wq, gq, kT, v, wo)


def _ffn_kernel(h_ref, g_ref, wg_ref, wu_ref, wd_ref, o_ref, *, n_chunks):
    h = h_ref[...]
    hn = _rms_rows(h, g_ref[...]).astype(BF16)
    fc = wg_ref.shape[1] // n_chunks
    acc = h
    for c in range(n_chunks):
        cs = slice(c * fc, (c + 1) * fc)
        gate = _bdot(hn, wg_ref[:, cs])
        act = gate * jax.nn.sigmoid(gate) * _bdot(hn, wu_ref[:, cs])
        acc = acc + _bdot(act.astype(BF16), wd_ref[cs, :])
    o_ref[...] = acc


def _ffn(h2, g, wg, wu, wd):
    N, D = h2.shape
    tm = ROW_TILE
    return pl.pallas_call(
        functools.partial(_ffn_kernel, n_chunks=2),
        grid=(N // tm,),
        in_specs=[
            pl.BlockSpec((tm, D), lambda i: (i, 0)),
            _resident(g.shape), _resident(wg.shape), _resident(wu.shape), _resident(wd.shape),
        ],
        out_specs=pl.BlockSpec((tm, D), lambda i: (i, 0)),
        out_shape=jax.ShapeDtypeStruct((N, D), F32),
        compiler_params=_params(1),
        name="ffn",
    )(h2, g, wg, wu, wd)


def _conv_kernel(h_ref, g_ref, win_ref, bin_ref, dw_ref, dwb_ref, lng_ref, lnb_ref, wout_ref,
                 o_ref, cbuf_ref, shift_ref, *, tiles_per_seq):
    tm, D = h_ref.shape
    i = pl.program_id(0)
    h = h_ref[...]
    hh = _bdot(_rms_rows(h, g_ref[...]).astype(BF16), win_ref[...]) + bin_ref[...]
    glu = hh[:, :D] * jax.nn.sigmoid(hh[:, D:])

    @pl.when(i % tiles_per_seq == 0)
    def _():
        cbuf_ref[0:CONV_HIST, :] = jnp.zeros((CONV_HIST, D), F32)

    cbuf_ref[CONV_HIST:CONV_HIST + tm, :] = glu
    off = CONV_HIST - (CONV_WIDTH - 1)
    span = tm + CONV_HIST - SUBLANES
    for r in range(1, SUBLANES):
        shift_ref[r - 1] = cbuf_ref[r:r + span, :]
    blocks = []
    for cb in range(D // LANES):
        cs = slice(cb * LANES, (cb + 1) * LANES)
        acc = None
        for j in range(CONV_WIDTH):
            a, r = divmod(off + j, SUBLANES)
            src = cbuf_ref if r == 0 else shift_ref.at[r - 1]
            term = src[a * SUBLANES:a * SUBLANES + tm, cs] * dw_ref[j:j + 1, cs]
            acc = term if acc is None else acc + term
        blocks.append(acc)
    cbuf_ref[0:CONV_HIST, :] = glu[tm - CONV_HIST:tm, :]

    y = jnp.concatenate(blocks, axis=-1) + dwb_ref[...]
    mu = jnp.mean(y, axis=-1, keepdims=True)
    yc = y - mu
    var = jnp.mean(yc * yc, axis=-1, keepdims=True)
    yn = yc * lax.rsqrt(var + EPS) * lng_ref[...] + lnb_ref[...]
    act = yn * jax.nn.sigmoid(yn)
    o_ref[...] = h + _bdot(act.astype(BF16), wout_ref[...])


def _conv(h2, g, win, bin_, dw, dwb, lng, lnb, wout, *, T):
    N, D = h2.shape
    tm = CONV_TILE
    tps = T // tm
    consts = (g, win, bin_, dw, dwb, lng, lnb, wout)
    return pl.pallas_call(
        functools.partial(_conv_kernel, tiles_per_seq=tps),
        grid=(N // tm,),
        in_specs=[pl.BlockSpec((tm, D), lambda i: (i, 0))] + [_resident(c.shape) for c in consts],
        out_specs=pl.BlockSpec((tm, D), lambda i: (i, 0)),
        out_shape=jax.ShapeDtypeStruct((N, D), F32),
        scratch_shapes=[
            pltpu.VMEM((CONV_HIST + tm, D), F32),
            pltpu.VMEM((SUBLANES - 1, tm + CONV_HIST - SUBLANES, D), F32),
        ],
        compiler_params=_params(1),
        name="conv",
    )(h2, *consts)


def _rope_tables(T, rot):
    half = rot // 2
    inv = ROPE_THETA ** (-jnp.arange(half, dtype=F32) * 2.0 / rot)
    ang = jnp.arange(T).astype(F32)[:, None] * inv[None, :]
    return jnp.cos(ang).T, jnp.sin(ang).T


def kernel(x, mem, norm_mix, norm_cross, norm_mem, norm_ffn, w_in_ab, a_q_norm, a_k_norm, pool_w, pool_scale, w_out_ab, conv_w_in, conv_b_in, conv_dw_w, conv_dw_b, conv_ln_g, conv_ln_b, conv_w_out, cross_wq, cross_wk, cross_wv, cross_q_norm, cross_k_norm, cross_wo, ffn_w_gate, ffn_w_up, ffn_w_down):
    B, T, D = x.shape
    N = B * T
    depth = norm_mix.shape[0]
    aw = A_HEADS * HEAD_DIM
    iqw = IDX_HEADS * IDX_DIM
    pw = len(POOL_WINDOWS) * POOL_GROUP
    assert T % ROW_TILE == 0 and T % KEY_CHUNK == 0 and KEY_CHUNK % QUERY_BLOCK == 0
    assert T % CONV_TILE == 0 and CONV_TILE >= CONV_HIST
    row = lambda v: v.reshape(1, -1)

    kT_mem, v_mem = _mem_kv(mem, norm_mem[:, None, :], cross_wk.astype(BF16),
                            cross_wv.astype(BF16), cross_k_norm[:, None, :])
    cos16, sin16 = _rope_tables(T, HEAD_DIM // 4)
    cos8, sin8 = _rope_tables(T, IDX_DIM // 4)

    h = x.reshape(N, D)
    for l in range(depth):
        if l % 2 == 0:
            e = l // 2
            w_in = w_in_ab[e]
            o_u, o_iq, o_iw, o_ik = 3 * aw, 3 * aw + pw, 3 * aw + pw + iqw, 3 * aw + pw + iqw + IDX_HEADS
            wt = jnp.concatenate(
                [w_in[:, :3 * aw], w_in[:, o_iq:o_iw], w_in[:, o_ik:], w_in[:, o_iw:o_ik]],
                axis=1).T.astype(BF16)
            wu = w_in[:, o_u:o_iq].astype(BF16)
            qT, k, vT, iqT, ik, iwT, b2 = _front(
                h, row(norm_mix[l]), wt, wu, a_q_norm[e][:, None], a_k_norm[e][:, None],
                cos16, sin16, cos8, sin8, pool_w[e].astype(BF16), row(pool_scale[e]), B=B, T=T)
            a = _dsa(qT, k, vT, iqT, ik, iwT, B=B, T=T)
            w_out = w_out_ab[e].astype(BF16)
            h = _mixout(h, a.reshape(N, aw), b2, w_out[:aw], w_out[aw:])
        else:
            o = l // 2
            h = _conv(h, row(norm_mix[l]), conv_w_in[o].astype(BF16), row(conv_b_in[o]),
                      conv_dw_w[o], row(conv_dw_b[o]), row(conv_ln_g[o]), row(conv_ln_b[o]),
                      conv_w_out[o].astype(BF16), T=T)
        h = _cross(h, row(norm_cross[l]), cross_wq[l].astype(BF16), row(cross_q_norm[l]),
                   kT_mem[l], v_mem[l], cross_wo[l].astype(BF16), T=T)
        h = _ffn(h, row(norm_ffn[l]), ffn_w_gate[l].astype(BF16), ffn_w_up[l].astype(BF16),
                 ffn_w_down[l].astype(BF16))
    return h.reshape(B, T, D)
```

```python
import functools

import jax
import jax.numpy as jnp
from jax import lax
from jax.experimental import pallas as pl
from jax.experimental.pallas import tpu as pltpu

F32 = jnp.float32
BF16 = jnp.bfloat16

EPS = 1e-6
A_HEADS = 4
HEAD_DIM = 128
IDX_HEADS = 16
IDX_DIM = 64
TOPK_MAX = 256
ROPE_THETA = 500000.0
POOL_WINDOWS = (2, 4, 8, 16)
POOL_GROUP = 128
POOL_HIST = 16
C_HEADS = 4
CONV_WIDTH = 31
CONV_HIST = 32
SUBLANES = 8
LANES = 128
BF16_ROWS = 16

VMEM_LIMIT_BYTES = 56 * 1024 * 1024
QUERY_BLOCK = 256
KEY_CHUNK = 512
CONV_TILE = 256
ROW_TILE = 512
INT32_MIN = -2 ** 31
MASKED_LOGIT = -1e30
V_PAD = 16
V_ROWS = HEAD_DIM + V_PAD
LOG2_E = 1.4426950408889634
LOW_BITS_UNCHECKED = 7


def _params(n_axes):
    return pltpu.CompilerParams(
        dimension_semantics=("arbitrary",) * n_axes,
        vmem_limit_bytes=VMEM_LIMIT_BYTES)


def _resident(shape):
    zeros = (0,) * len(shape)
    return pl.BlockSpec(shape, lambda *_: zeros, pipeline_mode=pl.Buffered(1))


def _rms_rows(x, g):
    ms = jnp.mean(x * x, axis=-1, keepdims=True)
    return x * lax.rsqrt(ms + EPS) * g


def _bdot(a, b):
    return jnp.dot(a, b, preferred_element_type=F32)


def _rope_cols(y, cos, sin, half):
    y1, y2 = y[0:half], y[half:2 * half]
    return jnp.concatenate(
        [y1 * cos - y2 * sin, y2 * cos + y1 * sin, y[2 * half:]], axis=0)


def _front_kernel(x_ref, g_ref, wt_ref, wu_ref, gq_ref, gk_ref, cos16_ref, sin16_ref,
                  cos8_ref, sin8_ref, poolw_ref, pscale_ref,
                  qT_ref, k_ref, vT_ref, iqT_ref, ik_ref, iwT_ref, b_ref, ubuf_ref,
                  *, tiles_per_seq):
    tm = x_ref.shape[0]
    i = pl.program_id(0)
    aw = A_HEADS * HEAD_DIM
    iqw = IDX_HEADS * IDX_DIM

    xn = _rms_rows(x_ref[...], g_ref[...])
    xb = xn.astype(BF16)
    xT = xn.T.astype(BF16)
    cols = _bdot(wt_ref[...], xT)

    cos16, sin16 = cos16_ref[...], sin16_ref[...]
    cos8, sin8 = cos8_ref[...], sin8_ref[...]

    def qk_heads(base, gain):
        heads = []
        for h in range(A_HEADS):
            xh = cols[base + h * HEAD_DIM: base + (h + 1) * HEAD_DIM]
            ms = jnp.sum(xh * xh, axis=0, keepdims=True) * (1.0 / HEAD_DIM)
            y = xh * lax.rsqrt(ms + EPS) * gain
            heads.append(_rope_cols(y, cos16, sin16, HEAD_DIM // 8))
        return jnp.concatenate(heads, axis=0)

    qT_ref[0] = (qk_heads(0, gq_ref[...]) * (HEAD_DIM ** -0.5 * LOG2_E)).astype(BF16)
    k_ref[0] = qk_heads(aw, gk_ref[...]).T.astype(BF16)
    ones_blk = (lax.broadcasted_iota(jnp.int32, (V_PAD, tm), 0) == 0).astype(F32)
    v_rows = []
    for h in range(A_HEADS):
        v_rows += [cols[2 * aw + h * HEAD_DIM:2 * aw + (h + 1) * HEAD_DIM], ones_blk]
    vT_ref[0, 0] = jnp.concatenate(v_rows, axis=0).astype(BF16)

    base = 3 * aw
    iq_heads = [
        _rope_cols(cols[base + h * IDX_DIM: base + (h + 1) * IDX_DIM], cos8, sin8, IDX_DIM // 8)
        for h in range(IDX_HEADS)]
    iqT_ref[0] = jnp.concatenate(iq_heads, axis=0).astype(BF16)

    base = 3 * aw + iqw
    ikT = _rope_cols(cols[base:base + IDX_DIM], cos8, sin8, IDX_DIM // 8)
    ikT = jnp.concatenate([ikT, jnp.zeros_like(ikT)], axis=0)
    ik_ref[0] = ikT.T[:, :IDX_DIM].astype(BF16)

    base = 3 * aw + iqw + IDX_DIM
    iwT_ref[0] = cols[base:base + IDX_HEADS] * (IDX_HEADS ** -0.5 * IDX_DIM ** -0.5)

    u = _bdot(xb, wu_ref[...])
    t_in_seq = (i % tiles_per_seq) * tm

    @pl.when(i % tiles_per_seq == 0)
    def _():
        ubuf_ref[0:POOL_HIST, :] = jnp.zeros((POOL_HIST, u.shape[1]), F32)

    ubuf_ref[POOL_HIST:POOL_HIST + tm, :] = u
    t1 = (lax.broadcasted_iota(jnp.int32, (tm, 1), 0) + t_in_seq + 1).astype(F32)
    outs = []
    for g, w in enumerate(POOL_WINDOWS):
        cs = slice(g * POOL_GROUP, (g + 1) * POOL_GROUP)
        acc = u[:, cs]
        for j in range(1, w):
            acc = acc + ubuf_ref[POOL_HIST - j:POOL_HIST - j + tm, cs]
        pooled = acc / jnp.minimum(t1, float(w)) - u[:, cs]
        outs.append(_bdot(pooled.astype(BF16), poolw_ref[g]))
    b_ref[...] = (jnp.concatenate(outs, axis=-1) * pscale_ref[...]).astype(BF16)
    ubuf_ref[0:POOL_HIST, :] = u[tm - POOL_HIST:tm, :]


def _front(x2, g, wt, wu, gq, gk, cos16, sin16, cos8, sin8, poolw, pscale, *, B, T):
    N, D = x2.shape
    tm = KEY_CHUNK
    tps = T // tm
    aw = A_HEADS * HEAD_DIM
    iqw = IDX_HEADS * IDX_DIM
    pw = len(POOL_WINDOWS) * POOL_GROUP
    seq_map = lambda i: (i // tps, 0, i % tps)
    out_shape = (
        jax.ShapeDtypeStruct((B, aw, T), BF16),
        jax.ShapeDtypeStruct((B, T, aw), BF16),
        jax.ShapeDtypeStruct((B, tps, A_HEADS * V_ROWS, tm), BF16),
        jax.ShapeDtypeStruct((B, iqw, T), BF16),
        jax.ShapeDtypeStruct((B, T, IDX_DIM), BF16),
        jax.ShapeDtypeStruct((B, IDX_HEADS, T), F32),
        jax.ShapeDtypeStruct((N, pw), BF16),
    )
    return pl.pallas_call(
        functools.partial(_front_kernel, tiles_per_seq=tps),
        grid=(N // tm,),
        in_specs=[
            pl.BlockSpec((tm, D), lambda i: (i, 0)),
            _resident(g.shape), _resident(wt.shape), _resident(wu.shape),
            _resident(gq.shape), _resident(gk.shape),
            pl.BlockSpec((HEAD_DIM // 8, tm), lambda i: (0, i % tps)),
            pl.BlockSpec((HEAD_DIM // 8, tm), lambda i: (0, i % tps)),
            pl.BlockSpec((IDX_DIM // 8, tm), lambda i: (0, i % tps)),
            pl.BlockSpec((IDX_DIM // 8, tm), lambda i: (0, i % tps)),
            _resident(poolw.shape), _resident(pscale.shape),
        ],
        out_specs=(
            pl.BlockSpec((1, aw, tm), seq_map),
            pl.BlockSpec((1, tm, aw), lambda i: (i // tps, i % tps, 0)),
            pl.BlockSpec((1, 1, A_HEADS * V_ROWS, tm), lambda i: (i // tps, i % tps, 0, 0)),
            pl.BlockSpec((1, iqw, tm), seq_map),
            pl.BlockSpec((1, tm, IDX_DIM), lambda i: (i // tps, i % tps, 0)),
            pl.BlockSpec((1, IDX_HEADS, tm), seq_map),
            pl.BlockSpec((tm, pw), lambda i: (i, 0)),
        ),
        out_shape=out_shape,
        scratch_shapes=[pltpu.VMEM((POOL_HIST + tm, pw), F32)],
        compiler_params=_params(1),
        name="front",
    )(x2, g, wt, wu, gq, gk, cos16, sin16, cos8, sin8, poolw, pscale)


def _key_to_f32(c):
    bits = c ^ ((c >> 31) & 0x7FFFFFFF)
    return lax.bitcast_convert_type(bits, F32)


def _high_half(x):
    bits = lax.bitcast_convert_type(x, jnp.int32) & jnp.int32(-65536)
    return lax.bitcast_convert_type(bits, F32).astype(BF16)


def _dsa_kernel(iqT_ref, iwT_ref, qT_ref, ik_ref, k_ref, vT_ref, o_ref,
                s_ref, s16_ref, acc_ref, lg_ref, *, topk):
    qb, kc = QUERY_BLOCK, KEY_CHUNK
    qi = pl.program_id(1)
    nblk = ((qi + 1) * qb + kc - 1) // kc

    row_i = lax.broadcasted_iota(jnp.int32, (kc, qb), 0)
    col_i = lax.broadcasted_iota(jnp.int32, (kc, qb), 1)

    def score_body(j, carry):
        s0 = pl.multiple_of(j * kc, kc)
        lhs = ik_ref[0, pl.ds(s0, kc), :]
        for h in range(IDX_HEADS):
            rhs = iqT_ref[0, h * IDX_DIM:(h + 1) * IDX_DIM, :]
            val = jnp.maximum(_bdot(lhs, rhs), 0.0) * iwT_ref[0, h:h + 1, :]
            if h > 0:
                val = s_ref[pl.ds(s0, kc), :] + val
            if h == IDX_HEADS - 1:
                val = jnp.where(row_i + s0 <= col_i + qi * qb, val, -jnp.inf)
                s16_ref[pl.ds(s0, kc), :] = _high_half(val)
            s_ref[pl.ds(s0, kc), :] = val
        return carry

    lax.fori_loop(0, nblk, score_body, 0)

    def count_ge_high(cand16):
        def body(j, acc):
            s0 = pl.multiple_of(j * kc, kc)
            blk = s16_ref[pl.ds(s0, kc), :]
            one = jnp.ones((), BF16)
            parts = [jnp.zeros((BF16_ROWS, qb), BF16) for _ in range(4)]
            for r in range(kc // BF16_ROWS):
                a = parts[r % 4]
                parts[r % 4] = jnp.where(blk[BF16_ROWS * r:BF16_ROWS * (r + 1)] >= cand16, a + one, a)
            part = (parts[0] + parts[1]) + (parts[2] + parts[3])
            return acc + part.astype(F32)

        acc = lax.fori_loop(0, nblk, body, jnp.zeros((BF16_ROWS, qb), F32))
        return jnp.sum(acc, axis=0, keepdims=True)

    def count_ge(cand):
        def body(j, accs):
            s0 = pl.multiple_of(j * kc, kc)
            blk = s_ref[pl.ds(s0, kc), :]
            accs = list(accs)
            for r in range(kc // SUBLANES):
                a = accs[r % len(accs)]
                accs[r % len(accs)] = jnp.where(blk[SUBLANES * r:SUBLANES * (r + 1)] >= cand, a + 1.0, a)
            return tuple(accs)

        accs = lax.fori_loop(0, nblk, body, tuple(jnp.zeros((SUBLANES, qb), F32) for _ in range(4)))
        return jnp.sum(accs[0] + accs[1] + accs[2] + accs[3], axis=0, keepdims=True)

    kf = float(topk)
    zero16 = jnp.zeros((1, qb), BF16)
    def step(c, settled, trial, cnt):
        c = jnp.where(jnp.logical_and(cnt >= kf, settled == 0.0), trial, c)
        return c, jnp.where(cnt == kf, 1.0, settled)

    c, settled = step(jnp.full((1, qb), INT32_MIN, jnp.int32), jnp.zeros((1, qb), F32),
                      jnp.zeros((1, qb), jnp.int32), count_ge_high(zero16))

    def high_bit_body(i, state):
        c, settled = state
        trial = c + jnp.left_shift(jnp.int32(1), 30 - i)
        return step(c, settled, trial, count_ge_high(_high_half(_key_to_f32(trial))))

    c, settled = lax.fori_loop(0, 15, high_bit_body, (c, settled))

    def low_bit_step(i, c, settled):
        trial = c + jnp.left_shift(jnp.int32(1), 15 - i)
        return step(c, settled, trial, count_ge(_key_to_f32(trial)))

    c, settled = lax.fori_loop(
        0, LOW_BITS_UNCHECKED, lambda i, st: low_bit_step(i, *st), (c, settled))

    def low_bit_cond(state):
        i, _, _, unsettled = state
        return jnp.logical_and(i < 16, unsettled > 0.0)

    def low_bit_body(state):
        i, c, settled, _ = state
        c, settled = low_bit_step(i, c, settled)
        return i + 1, c, settled, jnp.max(1.0 - settled)

    _, c, _, _ = lax.while_loop(
        low_bit_cond, low_bit_body,
        (jnp.int32(LOW_BITS_UNCHECKED), c, settled, jnp.max(1.0 - settled)))
    thr = jnp.where(c == INT32_MIN, -3.0e38, _key_to_f32(c))

    acc_ref[...] = jnp.zeros(acc_ref.shape, F32)

    def logits(jn):
        s0 = pl.multiple_of(jn * kc, kc)
        return [_bdot(k_ref[0, pl.ds(s0, kc), h * HEAD_DIM:(h + 1) * HEAD_DIM],
                      qT_ref[0, h * HEAD_DIM:(h + 1) * HEAD_DIM, :]) for h in range(A_HEADS)]

    def stage_logits(lgs):
        for h in range(A_HEADS):
            lg_ref[h * kc:(h + 1) * kc, :] = lgs[h]

    stage_logits(logits(0))

    def attn_body(j, ms):
        next_lgs = logits(jnp.minimum(j + 1, nblk - 1))
        s0 = pl.multiple_of(j * kc, kc)
        bias = jnp.where(s_ref[pl.ds(s0, kc), :] >= thr, 0.0, MASKED_LOGIT)
        new_ms = []
        for h in range(A_HEADS):
            vr = slice(h * V_ROWS, (h + 1) * V_ROWS)
            lg = lg_ref[h * kc:(h + 1) * kc, :] + bias
            m_new = jnp.maximum(ms[h], jnp.max(lg, axis=0, keepdims=True))
            p = jnp.exp2(lg - m_new).astype(BF16)
            acc_ref[vr, :] = jnp.exp2(ms[h] - m_new) * acc_ref[vr, :] + _bdot(vT_ref[0, j, vr, :], p)
            new_ms.append(m_new)
        stage_logits(next_lgs)
        return tuple(new_ms)

    lax.fori_loop(0, nblk, attn_body,
                  tuple(jnp.full((1, qb), MASKED_LOGIT, F32) for _ in range(A_HEADS)))
    outs = []
    for h in range(A_HEADS):
        denom = acc_ref[h * V_ROWS + HEAD_DIM:h * V_ROWS + HEAD_DIM + 1, :]
        outs.append(acc_ref[h * V_ROWS:h * V_ROWS + HEAD_DIM, :] * (1.0 / denom))
    o_ref[0] = jnp.concatenate(outs, axis=0).T.astype(BF16)


def _dsa(qT, k, vT, iqT, ik, iwT, *, B, T):
    qb, kc = QUERY_BLOCK, KEY_CHUNK
    aw = A_HEADS * HEAD_DIM
    iqw = IDX_HEADS * IDX_DIM
    topk = min(TOPK_MAX, T // 4)
    blk = lambda b, q: (b, 0, q)
    per_batch = lambda shape: pl.BlockSpec(
        (1,) + shape, lambda b, q: (b,) + (0,) * len(shape), pipeline_mode=pl.Buffered(1))
    return pl.pallas_call(
        functools.partial(_dsa_kernel, topk=topk),
        grid=(B, T // qb),
        in_specs=[
            pl.BlockSpec((1, iqw, qb), blk),
            pl.BlockSpec((1, IDX_HEADS, qb), blk),
            pl.BlockSpec((1, aw, qb), blk),
            per_batch((T, IDX_DIM)),
            per_batch((T, aw)),
            per_batch((T // kc, A_HEADS * V_ROWS, kc)),
        ],
        out_specs=pl.BlockSpec((1, qb, aw), lambda b, q: (b, q, 0)),
        out_shape=jax.ShapeDtypeStruct((B, T, aw), BF16),
        scratch_shapes=[
            pltpu.VMEM((T, qb), F32),
            pltpu.VMEM((T, qb), BF16),
            pltpu.VMEM((A_HEADS * V_ROWS, qb), F32),
            pltpu.VMEM((A_HEADS * kc, qb), F32),
        ],
        compiler_params=_params(2),
        name="dsa",
    )(iqT, iwT, qT, ik, k, vT)


def _mixout_kernel(x_ref, a_ref, b_ref, wa_ref, wb_ref, o_ref):
    o_ref[...] = x_ref[...] + _bdot(a_ref[...], wa_ref[...]) + _bdot(b_ref[...], wb_ref[...])


def _mixout(x2, a2, b2, wa, wb):
    N, D = x2.shape
    tm = ROW_TILE
    row = lambda w: pl.BlockSpec((tm, w), lambda i: (i, 0))
    return pl.pallas_call(
        _mixout_kernel,
        grid=(N // tm,),
        in_specs=[row(D), row(a2.shape[1]), row(b2.shape[1]), _resident(wa.shape), _resident(wb.shape)],
        out_specs=row(D),
        out_shape=jax.ShapeDtypeStruct((N, D), F32),
        compiler_params=_params(1),
        name="mixout",
    )(x2, a2, b2, wa, wb)


def _mem_kernel(mem_ref, g_ref, wk_ref, wv_ref, gk_ref, kT_ref, v_ref):
    dh = gk_ref.shape[-1]
    mn = _rms_rows(mem_ref[0], g_ref[0]).astype(BF16)
    k = _bdot(mn, wk_ref[0])
    v_ref[0, 0] = _bdot(mn, wv_ref[0]).astype(BF16)
    for c in range(C_HEADS):
        kc = _rms_rows(k[:, c * dh:(c + 1) * dh], gk_ref[0])
        kT_ref[0, 0, c] = kc.T.astype(BF16)


def _mem_kv(mem, g, wk, wv, gk):
    B, M, D = mem.shape
    L = wk.shape[0]
    dh = D // C_HEADS
    lay = lambda *tail: pl.BlockSpec((1,) + tail, lambda l, b: (l,) + (0,) * len(tail))
    return pl.pallas_call(
        _mem_kernel,
        grid=(L, B),
        in_specs=[
            pl.BlockSpec((1, M, D), lambda l, b: (b, 0, 0)),
            lay(1, D), lay(D, D), lay(D, D), lay(1, dh),
        ],
        out_specs=(
            pl.BlockSpec((1, 1, C_HEADS, dh, M), lambda l, b: (l, b, 0, 0, 0)),
            pl.BlockSpec((1, 1, M, D), lambda l, b: (l, b, 0, 0)),
        ),
        out_shape=(
            jax.ShapeDtypeStruct((L, B, C_HEADS, dh, M), BF16),
            jax.ShapeDtypeStruct((L, B, M, D), BF16),
        ),
        compiler_params=_params(2),
        name="mem_kv",
    )(mem, g, wk, wv, gk)


def _cross_kernel(h_ref, g_ref, wq_ref, gq_ref, kT_ref, v_ref, wo_ref, o_ref):
    dh = gq_ref.shape[-1]
    h = h_ref[...]
    q = _bdot(_rms_rows(h, g_ref[...]).astype(BF16), wq_ref[...])
    heads = []
    for c in range(C_HEADS):
        cs = slice(c * dh, (c + 1) * dh)
        qc = _rms_rows(q[:, cs], gq_ref[...]).astype(BF16)
        lg = _bdot(qc, kT_ref[0, c]) * dh ** -0.5
        p = jnp.exp(lg - jnp.max(lg, axis=-1, keepdims=True))
        p = p * (1.0 / jnp.sum(p, axis=-1, keepdims=True))
        heads.append(_bdot(p.astype(BF16), v_ref[0, :, cs]))
    o = jnp.concatenate(heads, axis=-1).astype(BF16)
    o_ref[...] = h + _bdot(o, wo_ref[...])


def _cross(h2, g, wq, gq, kT, v, wo, *, T):
    N, D = h2.shape
    tm = ROW_TILE
    tps = T // tm
    dh = D // C_HEADS
    M = v.shape[1]
    return pl.pallas_call(
        _cross_kernel,
        grid=(N // tm,),
        in_specs=[
            pl.BlockSpec((tm, D), lambda i: (i, 0)),
            _resident(g.shape), _resident(wq.shape), _resident(gq.shape),
            pl.BlockSpec((1, C_HEADS, dh, M), lambda i: (i // tps, 0, 0, 0)),
            pl.BlockSpec((1, M, D), lambda i: (i // tps, 0, 0)),
            _resident(wo.shape),
        ],
        out_specs=pl.BlockSpec((tm, D), lambda i: (i, 0)),
        out_shape=jax.ShapeDtypeStruct((N, D), F32),
        compiler_params=_params(1),
        name="cross",
    )(h2, g, wq, gq, kT, v, wo)


def _ffn_kernel(h_ref, g_ref, wg_ref, wu_ref, wd_ref, o_ref, *, n_chunks):
    h = h_ref[...]
    hn = _rms_rows(h, g_ref[...]).astype(BF16)
    fc = wg_ref.shape[1] // n_chunks
    acc = h
    for c in range(n_chunks):
        cs = slice(c * fc, (c + 1) * fc)
        gate = _bdot(hn, wg_ref[:, cs])
        act = gate * jax.nn.sigmoid(gate) * _bdot(hn, wu_ref[:, cs])
        acc = acc + _bdot(act.astype(BF16), wd_ref[cs, :])
    o_ref[...] = acc


def _ffn(h2, g, wg, wu, wd):
    N, D = h2.shape
    tm = ROW_TILE
    return pl.pallas_call(
        functools.partial(_ffn_kernel, n_chunks=2),
        grid=(N // tm,),
        in_specs=[
            pl.BlockSpec((tm, D), lambda i: (i, 0)),
            _resident(g.shape), _resident(wg.shape), _resident(wu.shape), _resident(wd.shape),
        ],
        out_specs=pl.BlockSpec((tm, D), lambda i: (i, 0)),
        out_shape=jax.ShapeDtypeStruct((N, D), F32),
        compiler_params=_params(1),
        name="ffn",
    )(h2, g, wg, wu, wd)


def _conv_kernel(h_ref, hprev_ref, g_ref, win_ref, bin_ref, dw_ref, dwb_ref, lng_ref, lnb_ref,
                 wout_ref, o_ref, cbuf_ref, shift_ref, gnext_ref, *, tiles_per_seq):
    tm, D = h_ref.shape
    i = pl.program_id(0)

    @pl.when(i == 0)
    def _():
        cbuf_ref[...] = jnp.zeros(cbuf_ref.shape, F32)

    @pl.when((i + tiles_per_seq - 1) % tiles_per_seq == 0)
    def _():
        cbuf_ref[0:CONV_HIST, :] = jnp.zeros((CONV_HIST, D), F32)

    hh = _bdot(_rms_rows(h_ref[...], g_ref[...]).astype(BF16), win_ref[...]) + bin_ref[...]
    gnext_ref[...] = hh[:, :D] * jax.nn.sigmoid(hh[:, D:])

    off = CONV_HIST - (CONV_WIDTH - 1)
    span = tm + CONV_HIST - SUBLANES
    for r in range(1, SUBLANES):
        shift_ref[r - 1] = cbuf_ref[r:r + span, :]
    blocks = []
    for cb in range(D // LANES):
        cs = slice(cb * LANES, (cb + 1) * LANES)
        acc = None
        for j in range(CONV_WIDTH):
            a, r = divmod(off + j, SUBLANES)
            src = cbuf_ref if r == 0 else shift_ref.at[r - 1]
            term = src[a * SUBLANES:a * SUBLANES + tm, cs] * dw_ref[j:j + 1, cs]
            acc = term if acc is None else acc + term
        blocks.append(acc)

    y = jnp.concatenate(blocks, axis=-1) + dwb_ref[...]
    mu = jnp.mean(y, axis=-1, keepdims=True)
    yc = y - mu
    var = jnp.mean(yc * yc, axis=-1, keepdims=True)
    yn = yc * lax.rsqrt(var + EPS) * lng_ref[...] + lnb_ref[...]
    act = yn * jax.nn.sigmoid(yn)
    o_ref[...] = hprev_ref[...] + _bdot(act.astype(BF16), wout_ref[...])

    cbuf_ref[0:CONV_HIST, :] = cbuf_ref[tm:tm + CONV_HIST, :]
    cbuf_ref[CONV_HIST:CONV_HIST + tm, :] = gnext_ref[...]


def _conv(h2, g, win, bin_, dw, dwb, lng, lnb, wout, *, T):
    N, D = h2.shape
    tm = CONV_TILE
    tps = T // tm
    consts = (g, win, bin_, dw, dwb, lng, lnb, wout)
    n_tiles = N // tm
    prev_tile = lambda i: (jnp.maximum(i - 1, 0), 0)
    return pl.pallas_call(
        functools.partial(_conv_kernel, tiles_per_seq=tps),
        grid=(n_tiles + 1,),
        in_specs=[pl.BlockSpec((tm, D), lambda i: (jnp.minimum(i, n_tiles - 1), 0)),
                  pl.BlockSpec((tm, D), prev_tile)] + [_resident(c.shape) for c in consts],
        out_specs=pl.BlockSpec((tm, D), prev_tile),
        out_shape=jax.ShapeDtypeStruct((N, D), F32),
        scratch_shapes=[
            pltpu.VMEM((CONV_HIST + tm, D), F32),
            pltpu.VMEM((SUBLANES - 1, tm + CONV_HIST - SUBLANES, D), F32),
            pltpu.VMEM((tm, D), F32),
        ],
        compiler_params=_params(1),
        name="conv",
    )(h2, h2, *consts)


def _rope_tables(T, rot):
    half = rot // 2
    inv = ROPE_THETA ** (-jnp.arange(half, dtype=F32) * 2.0 / rot)
    ang = jnp.arange(T).astype(F32)[:, None] * inv[None, :]
    return jnp.cos(ang).T, jnp.sin(ang).T


def kernel(x, mem, norm_mix, norm_cross, norm_mem, norm_ffn, w_in_ab, a_q_norm, a_k_norm, pool_w, pool_scale, w_out_ab, conv_w_in, conv_b_in, conv_dw_w, conv_dw_b, conv_ln_g, conv_ln_b, conv_w_out, cross_wq, cross_wk, cross_wv, cross_q_norm, cross_k_norm, cross_wo, ffn_w_gate, ffn_w_up, ffn_w_down):
    B, T, D = x.shape
    N = B * T
    depth = norm_mix.shape[0]
    aw = A_HEADS * HEAD_DIM
    iqw = IDX_HEADS * IDX_DIM
    pw = len(POOL_WINDOWS) * POOL_GROUP
    assert T % ROW_TILE == 0 and T % KEY_CHUNK == 0 and KEY_CHUNK % QUERY_BLOCK == 0
    assert T % CONV_TILE == 0 and CONV_TILE >= CONV_HIST
    row = lambda v: v.reshape(1, -1)

    kT_mem, v_mem = _mem_kv(mem, norm_mem[:, None, :], cross_wk.astype(BF16),
                            cross_wv.astype(BF16), cross_k_norm[:, None, :])
    cos16, sin16 = _rope_tables(T, HEAD_DIM // 4)
    cos8, sin8 = _rope_tables(T, IDX_DIM // 4)

    h = x.reshape(N, D)
    for l in range(depth):
        if l % 2 == 0:
            e = l // 2
            w_in = w_in_ab[e]
            o_u, o_iq, o_iw, o_ik = 3 * aw, 3 * aw + pw, 3 * aw + pw + iqw, 3 * aw + pw + iqw + IDX_HEADS
            wt = jnp.concatenate(
                [w_in[:, :3 * aw], w_in[:, o_iq:o_iw], w_in[:, o_ik:], w_in[:, o_iw:o_ik]],
                axis=1).T.astype(BF16)
            wu = w_in[:, o_u:o_iq].astype(BF16)
            qT, k, vT, iqT, ik, iwT, b2 = _front(
                h, row(norm_mix[l]), wt, wu, a_q_norm[e][:, None], a_k_norm[e][:, None],
                cos16, sin16, cos8, sin8, pool_w[e].astype(BF16), row(pool_scale[e]), B=B, T=T)
            a = _dsa(qT, k, vT, iqT, ik, iwT, B=B, T=T)
            w_out = w_out_ab[e].astype(BF16)
            h = _mixout(h, a.reshape(N, aw), b2, w_out[:aw], w_out[aw:])
        else:
            o = l // 2
            h = _conv(h, row(norm_mix[l]), conv_w_in[o].astype(BF16), row(conv_b_in[o]),
                      conv_dw_w[o], row(conv_dw_b[o]), row(conv_ln_g[o]), row(conv_ln_b[o]),
                      conv_w_out[o].astype(BF16), T=T)
        h = _cross(h, row(norm_cross[l]), cross_wq[l].astype(BF16), row(cross_q_norm[l]),
                   kT_mem[l], v_mem[l], cross_wo[l].astype(BF16), T=T)
        h = _ffn(h, row(norm_ffn[l]), ffn_w_gate[l].astype(BF16), ffn_w_up[l].astype(BF16),
                 ffn_w_down[l].astype(BF16))
    return h.reshape(B, T, D)
```

```python
import functools

import jax
import jax.numpy as jnp
from jax import lax
from jax.experimental import pallas as pl
from jax.experimental.pallas import tpu as pltpu

F32 = jnp.float32
BF16 = jnp.bfloat16

EPS = 1e-6
A_HEADS = 4
HEAD_DIM = 128
IDX_HEADS = 16
IDX_DIM = 64
TOPK_MAX = 256
ROPE_THETA = 500000.0
POOL_WINDOWS = (2, 4, 8, 16)
POOL_GROUP = 128
POOL_HIST = 16
C_HEADS = 4
CONV_WIDTH = 31
CONV_HIST = 32
SUBLANES = 8
LANES = 128
BF16_ROWS = 16

VMEM_LIMIT_BYTES = 56 * 1024 * 1024
QUERY_BLOCK = 256
KEY_CHUNK = 512
CONV_TILE = 256
ROW_TILE = 512
INT32_MIN = -2 ** 31
MASKED_LOGIT = -1e30
V_PAD = 16
V_ROWS = HEAD_DIM + V_PAD
LOG2_E = 1.4426950408889634
LOW_BITS_UNCHECKED = 7


def _params(n_axes):
    return pltpu.CompilerParams(
        dimension_semantics=("arbitrary",) * n_axes,
        vmem_limit_bytes=VMEM_LIMIT_BYTES)


def _resident(shape):
    zeros = (0,) * len(shape)
    return pl.BlockSpec(shape, lambda *_: zeros, pipeline_mode=pl.Buffered(1))


def _rms_rows(x, g):
    ms = jnp.mean(x * x, axis=-1, keepdims=True)
    return x * lax.rsqrt(ms + EPS) * g


def _bdot(a, b):
    return jnp.dot(a, b, preferred_element_type=F32)


def _rope_cols(y, cos, sin, half):
    y1, y2 = y[0:half], y[half:2 * half]
    return jnp.concatenate(
        [y1 * cos - y2 * sin, y2 * cos + y1 * sin, y[2 * half:]], axis=0)


def _front_kernel(x_ref, g_ref, wt_ref, wu_ref, gq_ref, gk_ref, cos16_ref, sin16_ref,
                  cos8_ref, sin8_ref, poolw_ref, pscale_ref,
                  qT_ref, k_ref, vT_ref, iqT_ref, ik_ref, iwT_ref, b_ref, ubuf_ref,
                  *, tiles_per_seq):
    tm = x_ref.shape[0]
    i = pl.program_id(0)
    aw = A_HEADS * HEAD_DIM
    iqw = IDX_HEADS * IDX_DIM

    xn = _rms_rows(x_ref[...], g_ref[...])
    xb = xn.astype(BF16)
    xT = xn.T.astype(BF16)
    cols = _bdot(wt_ref[...], xT)

    cos16, sin16 = cos16_ref[...], sin16_ref[...]
    cos8, sin8 = cos8_ref[...], sin8_ref[...]

    def qk_heads(base, gain):
        heads = []
        for h in range(A_HEADS):
            xh = cols[base + h * HEAD_DIM: base + (h + 1) * HEAD_DIM]
            ms = jnp.sum(xh * xh, axis=0, keepdims=True) * (1.0 / HEAD_DIM)
            y = xh * lax.rsqrt(ms + EPS) * gain
            heads.append(_rope_cols(y, cos16, sin16, HEAD_DIM // 8))
        return jnp.concatenate(heads, axis=0)

    qT_ref[0] = (qk_heads(0, gq_ref[...]) * (HEAD_DIM ** -0.5 * LOG2_E)).astype(BF16)
    k_ref[0] = qk_heads(aw, gk_ref[...]).T.astype(BF16)
    ones_blk = (lax.broadcasted_iota(jnp.int32, (V_PAD, tm), 0) == 0).astype(F32)
    v_rows = []
    for h in range(A_HEADS):
        v_rows += [cols[2 * aw + h * HEAD_DIM:2 * aw + (h + 1) * HEAD_DIM], ones_blk]
    vT_ref[0, 0] = jnp.concatenate(v_rows, axis=0).astype(BF16)

    base = 3 * aw
    iq_heads = [
        _rope_cols(cols[base + h * IDX_DIM: base + (h + 1) * IDX_DIM], cos8, sin8, IDX_DIM // 8)
        for h in range(IDX_HEADS)]
    iqT_ref[0] = jnp.concatenate(iq_heads, axis=0).astype(BF16)

    base = 3 * aw + iqw
    ikT = _rope_cols(cols[base:base + IDX_DIM], cos8, sin8, IDX_DIM // 8)
    ikT = jnp.concatenate([ikT, jnp.zeros_like(ikT)], axis=0)
    ik_ref[0] = ikT.T[:, :IDX_DIM].astype(BF16)

    base = 3 * aw + iqw + IDX_DIM
    iwT_ref[0] = cols[base:base + IDX_HEADS] * (IDX_HEADS ** -0.5 * IDX_DIM ** -0.5)

    u = _bdot(xb, wu_ref[...])
    t_in_seq = (i % tiles_per_seq) * tm

    @pl.when(i % tiles_per_seq == 0)
    def _():
        ubuf_ref[0:POOL_HIST, :] = jnp.zeros((POOL_HIST, u.shape[1]), F32)

    ubuf_ref[POOL_HIST:POOL_HIST + tm, :] = u
    t1 = (lax.broadcasted_iota(jnp.int32, (tm, 1), 0) + t_in_seq + 1).astype(F32)
    outs = []
    for g, w in enumerate(POOL_WINDOWS):
        cs = slice(g * POOL_GROUP, (g + 1) * POOL_GROUP)
        acc = u[:, cs]
        for j in range(1, w):
            acc = acc + ubuf_ref[POOL_HIST - j:POOL_HIST - j + tm, cs]
        pooled = acc / jnp.minimum(t1, float(w)) - u[:, cs]
        outs.append(_bdot(pooled.astype(BF16), poolw_ref[g]))
    b_ref[...] = (jnp.concatenate(outs, axis=-1) * pscale_ref[...]).astype(BF16)
    ubuf_ref[0:POOL_HIST, :] = u[tm - POOL_HIST:tm, :]


def _front(x2, g, wt, wu, gq, gk, cos16, sin16, cos8, sin8, poolw, pscale, *, B, T):
    N, D = x2.shape
    tm = KEY_CHUNK
    tps = T // tm
    aw = A_HEADS * HEAD_DIM
    iqw = IDX_HEADS * IDX_DIM
    pw = len(POOL_WINDOWS) * POOL_GROUP
    seq_map = lambda i: (i // tps, 0, i % tps)
    out_shape = (
        jax.ShapeDtypeStruct((B, aw, T), BF16),
        jax.ShapeDtypeStruct((B, T, aw), BF16),
        jax.ShapeDtypeStruct((B, tps, A_HEADS * V_ROWS, tm), BF16),
        jax.ShapeDtypeStruct((B, iqw, T), BF16),
        jax.ShapeDtypeStruct((B, T, IDX_DIM), BF16),
        jax.ShapeDtypeStruct((B, IDX_HEADS, T), F32),
        jax.ShapeDtypeStruct((N, pw), BF16),
    )
    return pl.pallas_call(
        functools.partial(_front_kernel, tiles_per_seq=tps),
        grid=(N // tm,),
        in_specs=[
            pl.BlockSpec((tm, D), lambda i: (i, 0)),
            _resident(g.shape), _resident(wt.shape), _resident(wu.shape),
            _resident(gq.shape), _resident(gk.shape),
            pl.BlockSpec((HEAD_DIM // 8, tm), lambda i: (0, i % tps)),
            pl.BlockSpec((HEAD_DIM // 8, tm), lambda i: (0, i % tps)),
            pl.BlockSpec((IDX_DIM // 8, tm), lambda i: (0, i % tps)),
            pl.BlockSpec((IDX_DIM // 8, tm), lambda i: (0, i % tps)),
            _resident(poolw.shape), _resident(pscale.shape),
        ],
        out_specs=(
            pl.BlockSpec((1, aw, tm), seq_map),
            pl.BlockSpec((1, tm, aw), lambda i: (i // tps, i % tps, 0)),
            pl.BlockSpec((1, 1, A_HEADS * V_ROWS, tm), lambda i: (i // tps, i % tps, 0, 0)),
            pl.BlockSpec((1, iqw, tm), seq_map),
            pl.BlockSpec((1, tm, IDX_DIM), lambda i: (i // tps, i % tps, 0)),
            pl.BlockSpec((1, IDX_HEADS, tm), seq_map),
            pl.BlockSpec((tm, pw), lambda i: (i, 0)),
        ),
        out_shape=out_shape,
        scratch_shapes=[pltpu.VMEM((POOL_HIST + tm, pw), F32)],
        compiler_params=_params(1),
        name="front",
    )(x2, g, wt, wu, gq, gk, cos16, sin16, cos8, sin8, poolw, pscale)


def _key_to_f32(c):
    bits = c ^ ((c >> 31) & 0x7FFFFFFF)
    return lax.bitcast_convert_type(bits, F32)


def _high_half(x):
    bits = lax.bitcast_convert_type(x, jnp.int32) & jnp.int32(-65536)
    return lax.bitcast_convert_type(bits, F32).astype(BF16)


def _dsa_kernel(iqT_ref, iwT_ref, qT_ref, ik_ref, k_ref, vT_ref, o_ref,
                s_ref, s16_ref, acc_ref, lg_ref, *, topk):
    qb, kc = QUERY_BLOCK, KEY_CHUNK
    row_bits = (s_ref.shape[0] - 1).bit_length()
    qi = pl.program_id(1)
    nblk = ((qi + 1) * qb + kc - 1) // kc

    row_i = lax.broadcasted_iota(jnp.int32, (kc, qb), 0)
    col_i = lax.broadcasted_iota(jnp.int32, (kc, qb), 1)

    def score_body(j, carry):
        s0 = pl.multiple_of(j * kc, kc)
        lhs = ik_ref[0, pl.ds(s0, kc), :]
        for h in range(IDX_HEADS):
            rhs = iqT_ref[0, h * IDX_DIM:(h + 1) * IDX_DIM, :]
            val = jnp.maximum(_bdot(lhs, rhs), 0.0) * iwT_ref[0, h:h + 1, :]
            if h > 0:
                val = s_ref[pl.ds(s0, kc), :] + val
            if h == IDX_HEADS - 1:
                val = jnp.where(row_i + s0 <= col_i + qi * qb, val, -jnp.inf)
                s16_ref[pl.ds(s0, kc), :] = _high_half(val)
            s_ref[pl.ds(s0, kc), :] = val
        return carry

    lax.fori_loop(0, nblk, score_body, 0)

    def count_ge_high(cand16):
        def body(j, acc):
            s0 = pl.multiple_of(j * kc, kc)
            blk = s16_ref[pl.ds(s0, kc), :]
            one = jnp.ones((), BF16)
            parts = [jnp.zeros((BF16_ROWS, qb), BF16) for _ in range(4)]
            for r in range(kc // BF16_ROWS):
                a = parts[r % 4]
                parts[r % 4] = jnp.where(blk[BF16_ROWS * r:BF16_ROWS * (r + 1)] >= cand16, a + one, a)
            part = (parts[0] + parts[1]) + (parts[2] + parts[3])
            return acc + part.astype(F32)

        acc = lax.fori_loop(0, nblk, body, jnp.zeros((BF16_ROWS, qb), F32))
        return jnp.sum(acc, axis=0, keepdims=True)

    def count_ge(cand):
        def body(j, accs):
            s0 = pl.multiple_of(j * kc, kc)
            blk = s_ref[pl.ds(s0, kc), :]
            accs = list(accs)
            for r in range(kc // SUBLANES):
                a = accs[r % len(accs)]
                accs[r % len(accs)] = jnp.where(blk[SUBLANES * r:SUBLANES * (r + 1)] >= cand, a + 1.0, a)
            return tuple(accs)

        accs = lax.fori_loop(0, nblk, body, tuple(jnp.zeros((SUBLANES, qb), F32) for _ in range(4)))
        return jnp.sum(accs[0] + accs[1] + accs[2] + accs[3], axis=0, keepdims=True)

    kf = float(topk)
    def step(c, settled, trial, cnt):
        c = jnp.where(jnp.logical_and(cnt >= kf, settled == 0.0), trial, c)
        return c, jnp.where(cnt == kf, 1.0, settled)

    c, settled = step(jnp.full((1, qb), INT32_MIN, jnp.int32), jnp.zeros((1, qb), F32),
                      jnp.zeros((1, qb), jnp.int32), count_ge_high(jnp.zeros((1, qb), BF16)))

    def high_bit_body(i, state):
        c, settled = state
        trial = c + jnp.left_shift(jnp.int32(1), 30 - i)
        return step(c, settled, trial, count_ge_high(_high_half(_key_to_f32(trial))))

    c, settled = lax.fori_loop(0, 15, high_bit_body, (c, settled))

    def low_bit_step(i, c, settled):
        trial = c + jnp.left_shift(jnp.int32(1), 15 - i)
        return step(c, settled, trial, count_ge(_key_to_f32(trial)))

    c, settled = lax.fori_loop(
        0, LOW_BITS_UNCHECKED, lambda i, st: low_bit_step(i, *st), (c, settled))

    def low_bit_cond(state):
        i, _, _, unsettled = state
        return jnp.logical_and(i < 16, unsettled > 0.0)

    def low_bit_body(state):
        i, c, settled, _ = state
        c, settled = low_bit_step(i, c, settled)
        return i + 1, c, settled, jnp.max(1.0 - settled)

    _, c, settled, _ = lax.while_loop(
        low_bit_cond, low_bit_body,
        (jnp.int32(LOW_BITS_UNCHECKED), c, settled, jnp.max(1.0 - settled)))
    thr = jnp.where(c == INT32_MIN, -3.0e38, _key_to_f32(c))

    tied = jnp.logical_and(settled == 0.0, c != INT32_MIN)

    @pl.when(jnp.max(jnp.where(tied, 1.0, 0.0)) > 0.0)
    def _():
        def count_rows(pred):
            def body(j, acc):
                s0 = pl.multiple_of(j * kc, kc)
                m = jnp.where(pred(s_ref[pl.ds(s0, kc), :], row_i + s0), 1.0, 0.0)
                return acc + jnp.sum(m, axis=0, keepdims=True)

            return lax.fori_loop(0, nblk, body, jnp.zeros((1, qb), F32))

        need = kf - count_rows(lambda blk, rows: blk > thr)

        def row_bit_body(i, r):
            trial = r + jnp.left_shift(jnp.int32(1), row_bits - 1 - i)
            before = count_rows(lambda blk, rows: jnp.logical_and(blk == thr, rows < trial))
            return jnp.where(before < need, trial, r)

        last = lax.fori_loop(0, row_bits, row_bit_body, jnp.zeros((1, qb), jnp.int32))

        def demote_body(j, carry):
            s0 = pl.multiple_of(j * kc, kc)
            blk = s_ref[pl.ds(s0, kc), :]
            drop = jnp.logical_and(jnp.logical_and(blk == thr, row_i + s0 > last), tied)
            s_ref[pl.ds(s0, kc), :] = jnp.where(drop, -jnp.inf, blk)
            return carry

        lax.fori_loop(0, nblk, demote_body, 0)

    acc_ref[...] = jnp.zeros(acc_ref.shape, F32)

    def logits(jn):
        s0 = pl.multiple_of(jn * kc, kc)
        return [_bdot(k_ref[0, pl.ds(s0, kc), h * HEAD_DIM:(h + 1) * HEAD_DIM],
                      qT_ref[0, h * HEAD_DIM:(h + 1) * HEAD_DIM, :]) for h in range(A_HEADS)]

    def stage_logits(lgs):
        for h in range(A_HEADS):
            lg_ref[h * kc:(h + 1) * kc, :] = lgs[h]

    stage_logits(logits(0))

    def attn_body(j, ms):
        next_lgs = logits(jnp.minimum(j + 1, nblk - 1))
        s0 = pl.multiple_of(j * kc, kc)
        bias = jnp.where(s_ref[pl.ds(s0, kc), :] >= thr, 0.0, MASKED_LOGIT)
        new_ms = []
        for h in range(A_HEADS):
            vr = slice(h * V_ROWS, (h + 1) * V_ROWS)
            lg = lg_ref[h * kc:(h + 1) * kc, :] + bias
            m_new = jnp.maximum(ms[h], jnp.max(lg, axis=0, keepdims=True))
            p = jnp.exp2(lg - m_new).astype(BF16)
            acc_ref[vr, :] = jnp.exp2(ms[h] - m_new) * acc_ref[vr, :] + _bdot(vT_ref[0, j, vr, :], p)
            new_ms.append(m_new)
        stage_logits(next_lgs)
        return tuple(new_ms)

    lax.fori_loop(0, nblk, attn_body,
                  tuple(jnp.full((1, qb), MASKED_LOGIT, F32) for _ in range(A_HEADS)))
    outs = []
    for h in range(A_HEADS):
        denom = acc_ref[h * V_ROWS + HEAD_DIM:h * V_ROWS + HEAD_DIM + 1, :]
        outs.append(acc_ref[h * V_ROWS:h * V_ROWS + HEAD_DIM, :] * (1.0 / denom))
    o_ref[0] = jnp.concatenate(outs, axis=0).T.astype(BF16)


def _dsa(qT, k, vT, iqT, ik, iwT, *, B, T):
    qb, kc = QUERY_BLOCK, KEY_CHUNK
    aw = A_HEADS * HEAD_DIM
    iqw = IDX_HEADS * IDX_DIM
    topk = min(TOPK_MAX, T // 4)
    blk = lambda b, q: (b, 0, q)
    per_batch = lambda shape: pl.BlockSpec(
        (1,) + shape, lambda b, q: (b,) + (0,) * len(shape), pipeline_mode=pl.Buffered(1))
    return pl.pallas_call(
        functools.partial(_dsa_kernel, topk=topk),
        grid=(B, T // qb),
        in_specs=[
            pl.BlockSpec((1, iqw, qb), blk),
            pl.BlockSpec((1, IDX_HEADS, qb), blk),
            pl.BlockSpec((1, aw, qb), blk),
            per_batch((T, IDX_DIM)),
            per_batch((T, aw)),
            per_batch((T // kc, A_HEADS * V_ROWS, kc)),
        ],
        out_specs=pl.BlockSpec((1, qb, aw), lambda b, q: (b, q, 0)),
        out_shape=jax.ShapeDtypeStruct((B, T, aw), BF16),
        scratch_shapes=[
            pltpu.VMEM((T, qb), F32),
            pltpu.VMEM((T, qb), BF16),
            pltpu.VMEM((A_HEADS * V_ROWS, qb), F32),
            pltpu.VMEM((A_HEADS * kc, qb), F32),
        ],
        compiler_params=_params(2),
        name="dsa",
    )(iqT, iwT, qT, ik, k, vT)


def _mixout_kernel(x_ref, a_ref, b_ref, wa_ref, wb_ref, o_ref):
    o_ref[...] = x_ref[...] + _bdot(a_ref[...], wa_ref[...]) + _bdot(b_ref[...], wb_ref[...])


def _mixout(x2, a2, b2, wa, wb):
    N, D = x2.shape
    tm = ROW_TILE
    row = lambda w: pl.BlockSpec((tm, w), lambda i: (i, 0))
    return pl.pallas_call(
        _mixout_kernel,
        grid=(N // tm,),
        in_specs=[row(D), row(a2.shape[1]), row(b2.shape[1]), _resident(wa.shape), _resident(wb.shape)],
        out_specs=row(D),
        out_shape=jax.ShapeDtypeStruct((N, D), F32),
        compiler_params=_params(1),
        name="mixout",
    )(x2, a2, b2, wa, wb)


def _mem_kernel(mem_ref, g_ref, wk_ref, wv_ref, gk_ref, kT_ref, v_ref):
    dh = gk_ref.shape[-1]
    mn = _rms_rows(mem_ref[0], g_ref[0]).astype(BF16)
    k = _bdot(mn, wk_ref[0])
    v_ref[0, 0] = _bdot(mn, wv_ref[0]).astype(BF16)
    for c in range(C_HEADS):
        kc = _rms_rows(k[:, c * dh:(c + 1) * dh], gk_ref[0])
        kT_ref[0, 0, c] = kc.T.astype(BF16)


def _mem_kv(mem, g, wk, wv, gk):
    B, M, D = mem.shape
    L = wk.shape[0]
    dh = D // C_HEADS
    lay = lambda *tail: pl.BlockSpec((1,) + tail, lambda l, b: (l,) + (0,) * len(tail))
    return pl.pallas_call(
        _mem_kernel,
        grid=(L, B),
        in_specs=[
            pl.BlockSpec((1, M, D), lambda l, b: (b, 0, 0)),
            lay(1, D), lay(D, D), lay(D, D), lay(1, dh),
        ],
        out_specs=(
            pl.BlockSpec((1, 1, C_HEADS, dh, M), lambda l, b: (l, b, 0, 0, 0)),
            pl.BlockSpec((1, 1, M, D), lambda l, b: (l, b, 0, 0)),
        ),
        out_shape=(
            jax.ShapeDtypeStruct((L, B, C_HEADS, dh, M), BF16),
            jax.ShapeDtypeStruct((L, B, M, D), BF16),
        ),
        compiler_params=_params(2),
        name="mem_kv",
    )(mem, g, wk, wv, gk)


def _cross_kernel(h_ref, g_ref, wq_ref, gq_ref, kT_ref, v_ref, wo_ref, o_ref):
    dh = gq_ref.shape[-1]
    h = h_ref[...]
    q = _bdot(_rms_rows(h, g_ref[...]).astype(BF16), wq_ref[...])
    heads = []
    for c in range(C_HEADS):
        cs = slice(c * dh, (c + 1) * dh)
        qc = _rms_rows(q[:, cs], gq_ref[...]).astype(BF16)
        lg = _bdot(qc, kT_ref[0, c]) * dh ** -0.5
        p = jnp.exp(lg - jnp.max(lg, axis=-1, keepdims=True))
        p = p * (1.0 / jnp.sum(p, axis=-1, keepdims=True))
        heads.append(_bdot(p.astype(BF16), v_ref[0, :, cs]))
    o = jnp.concatenate(heads, axis=-1).astype(BF16)
    o_ref[...] = h + _bdot(o, wo_ref[...])


def _cross(h2, g, wq, gq, kT, v, wo, *, T):
    N, D = h2.shape
    tm = ROW_TILE
    tps = T // tm
    dh = D // C_HEADS
    M = v.shape[1]
    return pl.pallas_call(
        _cross_kernel,
        grid=(N // tm,),
        in_specs=[
            pl.BlockSpec((tm, D), lambda i: (i, 0)),
            _resident(g.shape), _resident(wq.shape), _resident(gq.shape),
            pl.BlockSpec((1, C_HEADS, dh, M), lambda i: (i // tps, 0, 0, 0)),
            pl.BlockSpec((1, M, D), lambda i: (i // tps, 0, 0)),
            _resident(wo.shape),
        ],
        out_specs=pl.BlockSpec((tm, D), lambda i: (i, 0)),
        out_shape=jax.ShapeDtypeStruct((N, D), F32),
        compiler_params=_params(1),
        name="cross",
    )(h2, g, wq, gq, kT, v, wo)


def _ffn_kernel(h_ref, g_ref, wg_ref, wu_ref, wd_ref, o_ref, *, n_chunks):
    h = h_ref[...]
    hn = _rms_rows(h, g_ref[...]).astype(BF16)
    fc = wg_ref.shape[1] // n_chunks
    acc = h
    for c in range(n_chunks):
        cs = slice(c * fc, (c + 1) * fc)
        gate = _bdot(hn, wg_ref[:, cs])
        act = gate * jax.nn.sigmoid(gate) * _bdot(hn, wu_ref[:, cs])
        acc = acc + _bdot(act.astype(BF16), wd_ref[cs, :])
    o_ref[...] = acc


def _ffn(h2, g, wg, wu, wd):
    N, D = h2.shape
    tm = ROW_TILE
    return pl.pallas_call(
        functools.partial(_ffn_kernel, n_chunks=2),
        grid=(N // tm,),
        in_specs=[
            pl.BlockSpec((tm, D), lambda i: (i, 0)),
            _resident(g.shape), _resident(wg.shape), _resident(wu.shape), _resident(wd.shape),
        ],
        out_specs=pl.BlockSpec((tm, D), lambda i: (i, 0)),
        out_shape=jax.ShapeDtypeStruct((N, D), F32),
        compiler_params=_params(1),
        name="ffn",
    )(h2, g, wg, wu, wd)


def _conv_kernel(h_ref, hprev_ref, g_ref, win_ref, bin_ref, dw_ref, dwb_ref, lng_ref, lnb_ref,
                 wout_ref, o_ref, cbuf_ref, shift_ref, gnext_ref, *, tiles_per_seq):
    tm, D = h_ref.shape
    i = pl.program_id(0)

    @pl.when(i == 0)
    def _():
        cbuf_ref[...] = jnp.zeros(cbuf_ref.shape, F32)

    @pl.when((i + tiles_per_seq - 1) % tiles_per_seq == 0)
    def _():
        cbuf_ref[0:CONV_HIST, :] = jnp.zeros((CONV_HIST, D), F32)

    hh = _bdot(_rms_rows(h_ref[...], g_ref[...]).astype(BF16), win_ref[...]) + bin_ref[...]
    gnext_ref[...] = hh[:, :D] * jax.nn.sigmoid(hh[:, D:])

    off = CONV_HIST - (CONV_WIDTH - 1)
    span = tm + CONV_HIST - SUBLANES
    for r in range(1, SUBLANES):
        shift_ref[r - 1] = cbuf_ref[r:r + span, :]
    blocks = []
    for cb in range(D // LANES):
        cs = slice(cb * LANES, (cb + 1) * LANES)
        acc = None
        for j in range(CONV_WIDTH):
            a, r = divmod(off + j, SUBLANES)
            src = cbuf_ref if r == 0 else shift_ref.at[r - 1]
            term = src[a * SUBLANES:a * SUBLANES + tm, cs] * dw_ref[j:j + 1, cs]
            acc = term if acc is None else acc + term
        blocks.append(acc)

    y = jnp.concatenate(blocks, axis=-1) + dwb_ref[...]
    mu = jnp.mean(y, axis=-1, keepdims=True)
    yc = y - mu
    var = jnp.mean(yc * yc, axis=-1, keepdims=True)
    yn = yc * lax.rsqrt(var + EPS) * lng_ref[...] + lnb_ref[...]
    act = yn * jax.nn.sigmoid(yn)
    o_ref[...] = hprev_ref[...] + _bdot(act.astype(BF16), wout_ref[...])

    cbuf_ref[0:CONV_HIST, :] = cbuf_ref[tm:tm + CONV_HIST, :]
    cbuf_ref[CONV_HIST:CONV_HIST + tm, :] = gnext_ref[...]


def _conv(h2, g, win, bin_, dw, dwb, lng, lnb, wout, *, T):
    N, D = h2.shape
    tm = CONV_TILE
    tps = T // tm
    consts = (g, win, bin_, dw, dwb, lng, lnb, wout)
    n_tiles = N // tm
    prev_tile = lambda i: (jnp.maximum(i - 1, 0), 0)
    return pl.pallas_call(
        functools.partial(_conv_kernel, tiles_per_seq=tps),
        grid=(n_tiles + 1,),
        in_specs=[pl.BlockSpec((tm, D), lambda i: (jnp.minimum(i, n_tiles - 1), 0)),
                  pl.BlockSpec((tm, D), prev_tile)] + [_resident(c.shape) for c in consts],
        out_specs=pl.BlockSpec((tm, D), prev_tile),
        out_shape=jax.ShapeDtypeStruct((N, D), F32),
        scratch_shapes=[
            pltpu.VMEM((CONV_HIST + tm, D), F32),
            pltpu.VMEM((SUBLANES - 1, tm + CONV_HIST - SUBLANES, D), F32),
            pltpu.VMEM((tm, D), F32),
        ],
        compiler_params=_params(1),
        name="conv",
    )(h2, h2, *consts)


def _rope_tables(T, rot):
    half = rot // 2
    inv = ROPE_THETA ** (-jnp.arange(half, dtype=F32) * 2.0 / rot)
    ang = jnp.arange(T).astype(F32)[:, None] * inv[None, :]
    return jnp.cos(ang).T, jnp.sin(ang).T


def kernel(x, mem, norm_mix, norm_cross, norm_mem, norm_ffn, w_in_ab, a_q_norm, a_k_norm, pool_w, pool_scale, w_out_ab, conv_w_in, conv_b_in, conv_dw_w, conv_dw_b, conv_ln_g, conv_ln_b, conv_w_out, cross_wq, cross_wk, cross_wv, cross_q_norm, cross_k_norm, cross_wo, ffn_w_gate, ffn_w_up, ffn_w_down):
    B, T, D = x.shape
    N = B * T
    depth = norm_mix.shape[0]
    aw = A_HEADS * HEAD_DIM
    iqw = IDX_HEADS * IDX_DIM
    pw = len(POOL_WINDOWS) * POOL_GROUP
    assert T % ROW_TILE == 0 and T % KEY_CHUNK == 0 and KEY_CHUNK % QUERY_BLOCK == 0
    assert T % CONV_TILE == 0 and CONV_TILE >= CONV_HIST
    row = lambda v: v.reshape(1, -1)

    kT_mem, v_mem = _mem_kv(mem, norm_mem[:, None, :], cross_wk.astype(BF16),
                            cross_wv.astype(BF16), cross_k_norm[:, None, :])
    cos16, sin16 = _rope_tables(T, HEAD_DIM // 4)
    cos8, sin8 = _rope_tables(T, IDX_DIM // 4)

    h = x.reshape(N, D)
    for l in range(depth):
        if l % 2 == 0:
            e = l // 2
            w_in = w_in_ab[e]
            o_u, o_iq, o_iw, o_ik = 3 * aw, 3 * aw + pw, 3 * aw + pw + iqw, 3 * aw + pw + iqw + IDX_HEADS
            wt = jnp.concatenate(
                [w_in[:, :3 * aw], w_in[:, o_iq:o_iw], w_in[:, o_ik:], w_in[:, o_iw:o_ik]],
                axis=1).T.astype(BF16)
            wu = w_in[:, o_u:o_iq].astype(BF16)
            qT, k, vT, iqT, ik, iwT, b2 = _front(
                h, row(norm_mix[l]), wt, wu, a_q_norm[e][:, None], a_k_norm[e][:, None],
                cos16, sin16, cos8, sin8, pool_w[e].astype(BF16), row(pool_scale[e]), B=B, T=T)
            a = _dsa(qT, k, vT, iqT, ik, iwT, B=B, T=T)
            w_out = w_out_ab[e].astype(BF16)
            h = _mixout(h, a.reshape(N, aw), b2, w_out[:aw], w_out[aw:])
        else:
            o = l // 2
            h = _conv(h, row(norm_mix[l]), conv_w_in[o].astype(BF16), row(conv_b_in[o]),
                      conv_dw_w[o], row(conv_dw_b[o]), row(conv_ln_g[o]), row(conv_ln_b[o]),
                      conv_w_out[o].astype(BF16), T=T)
        h = _cross(h, row(norm_cross[l]), cross_wq[l].astype(BF16), row(cross_q_norm[l]),
                   kT_mem[l], v_mem[l], cross_wo[l].astype(BF16), T=T)
        h = _ffn(h, row(norm_ffn[l]), ffn_w_gate[l].astype(BF16), ffn_w_up[l].astype(BF16),
                 ffn_w_down[l].astype(BF16))
    return h.reshape(B, T, D)
```

```python
import functools

import jax
import jax.numpy as jnp
from jax import lax
from jax.experimental import pallas as pl
from jax.experimental.pallas import tpu as pltpu

F32 = jnp.float32
BF16 = jnp.bfloat16

EPS = 1e-6
A_HEADS = 4
HEAD_DIM = 128
IDX_HEADS = 16
IDX_DIM = 64
TOPK_MAX = 256
ROPE_THETA = 500000.0
POOL_WINDOWS = (2, 4, 8, 16)
POOL_GROUP = 128
POOL_HIST = 16
C_HEADS = 4
CONV_WIDTH = 31
CONV_HIST = 32
SUBLANES = 8
LANES = 128
BF16_ROWS = 16

VMEM_LIMIT_BYTES = 56 * 1024 * 1024
QUERY_BLOCK = 256
KEY_CHUNK = 512
CONV_TILE = 256
ROW_TILE = 512
INT32_MIN = -2 ** 31
MASKED_LOGIT = -1e30
V_PAD = 16
V_ROWS = HEAD_DIM + V_PAD
LOG2_E = 1.4426950408889634
LOW_BITS_UNCHECKED = 7


def _params(n_axes):
    return pltpu.CompilerParams(
        dimension_semantics=("arbitrary",) * n_axes,
        vmem_limit_bytes=VMEM_LIMIT_BYTES)


def _resident(shape):
    zeros = (0,) * len(shape)
    return pl.BlockSpec(shape, lambda *_: zeros, pipeline_mode=pl.Buffered(1))


def _rms_rows(x, g):
    ms = jnp.mean(x * x, axis=-1, keepdims=True)
    return x * lax.rsqrt(ms + EPS) * g


def _bdot(a, b):
    return jnp.dot(a, b, preferred_element_type=F32)


def _rope_cols(y, cos, sin, half):
    y1, y2 = y[0:half], y[half:2 * half]
    return jnp.concatenate(
        [y1 * cos - y2 * sin, y2 * cos + y1 * sin, y[2 * half:]], axis=0)


def _front_kernel(x_ref, g_ref, wt_ref, wu_ref, gq_ref, gk_ref, cos16_ref, sin16_ref,
                  cos8_ref, sin8_ref, poolw_ref, pscale_ref,
                  qT_ref, k_ref, vT_ref, iqT_ref, ik_ref, iwT_ref, b_ref, ubuf_ref,
                  *, tiles_per_seq):
    tm = x_ref.shape[0]
    i = pl.program_id(0)
    aw = A_HEADS * HEAD_DIM
    iqw = IDX_HEADS * IDX_DIM

    xn = _rms_rows(x_ref[...], g_ref[...])
    xb = xn.astype(BF16)
    xT = xn.T.astype(BF16)
    cols = _bdot(wt_ref[...], xT)

    cos16, sin16 = cos16_ref[...], sin16_ref[...]
    cos8, sin8 = cos8_ref[...], sin8_ref[...]

    def qk_heads(base, gain):
        heads = []
        for h in range(A_HEADS):
            xh = cols[base + h * HEAD_DIM: base + (h + 1) * HEAD_DIM]
            ms = jnp.sum(xh * xh, axis=0, keepdims=True) * (1.0 / HEAD_DIM)
            y = xh * lax.rsqrt(ms + EPS) * gain
            heads.append(_rope_cols(y, cos16, sin16, HEAD_DIM // 8))
        return jnp.concatenate(heads, axis=0)

    qT_ref[0] = (qk_heads(0, gq_ref[...]) * (HEAD_DIM ** -0.5 * LOG2_E)).astype(BF16)
    k_ref[0] = qk_heads(aw, gk_ref[...]).T.astype(BF16)
    ones_blk = (lax.broadcasted_iota(jnp.int32, (V_PAD, tm), 0) == 0).astype(F32)
    v_rows = []
    for h in range(A_HEADS):
        v_rows += [cols[2 * aw + h * HEAD_DIM:2 * aw + (h + 1) * HEAD_DIM], ones_blk]
    vT_ref[0, 0] = jnp.concatenate(v_rows, axis=0).astype(BF16)

    base = 3 * aw
    iq_heads = [
        _rope_cols(cols[base + h * IDX_DIM: base + (h + 1) * IDX_DIM], cos8, sin8, IDX_DIM // 8)
        for h in range(IDX_HEADS)]
    iqT_ref[0] = jnp.concatenate(iq_heads, axis=0).astype(BF16)

    base = 3 * aw + iqw
    ikT = _rope_cols(cols[base:base + IDX_DIM], cos8, sin8, IDX_DIM // 8)
    ikT = jnp.concatenate([ikT, jnp.zeros_like(ikT)], axis=0)
    ik_ref[0] = ikT.T[:, :IDX_DIM].astype(BF16)

    base = 3 * aw + iqw + IDX_DIM
    iwT_ref[0] = cols[base:base + IDX_HEADS] * (IDX_HEADS ** -0.5 * IDX_DIM ** -0.5)

    u = _bdot(xb, wu_ref[...])
    t_in_seq = (i % tiles_per_seq) * tm

    @pl.when(i % tiles_per_seq == 0)
    def _():
        ubuf_ref[0:POOL_HIST, :] = jnp.zeros((POOL_HIST, u.shape[1]), F32)

    ubuf_ref[POOL_HIST:POOL_HIST + tm, :] = u
    t1 = (lax.broadcasted_iota(jnp.int32, (tm, 1), 0) + t_in_seq + 1).astype(F32)
    outs = []
    for g, w in enumerate(POOL_WINDOWS):
        cs = slice(g * POOL_GROUP, (g + 1) * POOL_GROUP)
        acc = u[:, cs]
        for j in range(1, w):
            acc = acc + ubuf_ref[POOL_HIST - j:POOL_HIST - j + tm, cs]
        pooled = acc / jnp.minimum(t1, float(w)) - u[:, cs]
        outs.append(_bdot(pooled.astype(BF16), poolw_ref[g]))
    b_ref[...] = (jnp.concatenate(outs, axis=-1) * pscale_ref[...]).astype(BF16)
    ubuf_ref[0:POOL_HIST, :] = u[tm - POOL_HIST:tm, :]


def _front(x2, g, wt, wu, gq, gk, cos16, sin16, cos8, sin8, poolw, pscale, *, B, T):
    N, D = x2.shape
    tm = KEY_CHUNK
    tps = T // tm
    aw = A_HEADS * HEAD_DIM
    iqw = IDX_HEADS * IDX_DIM
    pw = len(POOL_WINDOWS) * POOL_GROUP
    seq_map = lambda i: (i // tps, 0, i % tps)
    out_shape = (
        jax.ShapeDtypeStruct((B, aw, T), BF16),
        jax.ShapeDtypeStruct((B, T, aw), BF16),
        jax.ShapeDtypeStruct((B, tps, A_HEADS * V_ROWS, tm), BF16),
        jax.ShapeDtypeStruct((B, iqw, T), BF16),
        jax.ShapeDtypeStruct((B, T, IDX_DIM), BF16),
        jax.ShapeDtypeStruct((B, IDX_HEADS, T), F32),
        jax.ShapeDtypeStruct((N, pw), BF16),
    )
    return pl.pallas_call(
        functools.partial(_front_kernel, tiles_per_seq=tps),
        grid=(N // tm,),
        in_specs=[
            pl.BlockSpec((tm, D), lambda i: (i, 0)),
            _resident(g.shape), _resident(wt.shape), _resident(wu.shape),
            _resident(gq.shape), _resident(gk.shape),
            pl.BlockSpec((HEAD_DIM // 8, tm), lambda i: (0, i % tps)),
            pl.BlockSpec((HEAD_DIM // 8, tm), lambda i: (0, i % tps)),
            pl.BlockSpec((IDX_DIM // 8, tm), lambda i: (0, i % tps)),
            pl.BlockSpec((IDX_DIM // 8, tm), lambda i: (0, i % tps)),
            _resident(poolw.shape), _resident(pscale.shape),
        ],
        out_specs=(
            pl.BlockSpec((1, aw, tm), seq_map),
            pl.BlockSpec((1, tm, aw), lambda i: (i // tps, i % tps, 0)),
            pl.BlockSpec((1, 1, A_HEADS * V_ROWS, tm), lambda i: (i // tps, i % tps, 0, 0)),
            pl.BlockSpec((1, iqw, tm), seq_map),
            pl.BlockSpec((1, tm, IDX_DIM), lambda i: (i // tps, i % tps, 0)),
            pl.BlockSpec((1, IDX_HEADS, tm), seq_map),
            pl.BlockSpec((tm, pw), lambda i: (i, 0)),
        ),
        out_shape=out_shape,
        scratch_shapes=[pltpu.VMEM((POOL_HIST + tm, pw), F32)],
        compiler_params=_params(1),
        name="front",
    )(x2, g, wt, wu, gq, gk, cos16, sin16, cos8, sin8, poolw, pscale)


def _key_to_f32(c):
    bits = c ^ ((c >> 31) & 0x7FFFFFFF)
    return lax.bitcast_convert_type(bits, F32)


def _high_half(x):
    bits = lax.bitcast_convert_type(x, jnp.int32) & jnp.int32(-65536)
    return lax.bitcast_convert_type(bits, F32).astype(BF16)


def _dsa_kernel(iqT_ref, iwT_ref, qT_ref, ik_ref, k_ref, vT_ref, o_ref,
                s_ref, s16_ref, acc_ref, lg_ref, *, topk):
    qb, kc = QUERY_BLOCK, KEY_CHUNK
    row_bits = (s_ref.shape[0] - 1).bit_length()
    qi = pl.program_id(1)
    nblk = ((qi + 1) * qb + kc - 1) // kc

    row_i = lax.broadcasted_iota(jnp.int32, (kc, qb), 0)
    col_i = lax.broadcasted_iota(jnp.int32, (kc, qb), 1)

    def score_body(j, carry):
        s0 = pl.multiple_of(j * kc, kc)
        lhs = ik_ref[0, pl.ds(s0, kc), :]
        for h in range(IDX_HEADS):
            rhs = iqT_ref[0, h * IDX_DIM:(h + 1) * IDX_DIM, :]
            val = jnp.maximum(_bdot(lhs, rhs), 0.0) * iwT_ref[0, h:h + 1, :]
            if h > 0:
                val = s_ref[pl.ds(s0, kc), :] + val
            if h == IDX_HEADS - 1:
                val = jnp.where(row_i + s0 <= col_i + qi * qb, val, -jnp.inf)
                s16_ref[pl.ds(s0, kc), :] = _high_half(val)
            s_ref[pl.ds(s0, kc), :] = val
        return carry

    lax.fori_loop(0, nblk, score_body, 0)

    def count_ge_high(cand16):
        def body(j, acc):
            s0 = pl.multiple_of(j * kc, kc)
            blk = s16_ref[pl.ds(s0, kc), :]
            one = jnp.ones((), BF16)
            parts = [jnp.zeros((BF16_ROWS, qb), BF16) for _ in range(4)]
            for r in range(kc // BF16_ROWS):
                a = parts[r % 4]
                parts[r % 4] = jnp.where(blk[BF16_ROWS * r:BF16_ROWS * (r + 1)] >= cand16, a + one, a)
            part = (parts[0] + parts[1]) + (parts[2] + parts[3])
            return acc + part.astype(F32)

        acc = lax.fori_loop(0, nblk, body, jnp.zeros((BF16_ROWS, qb), F32))
        return jnp.sum(acc, axis=0, keepdims=True)

    def count_ge(cand):
        def body(j, accs):
            s0 = pl.multiple_of(j * kc, kc)
            blk = s_ref[pl.ds(s0, kc), :]
            accs = list(accs)
            for r in range(kc // SUBLANES):
                a = accs[r % len(accs)]
                accs[r % len(accs)] = jnp.where(blk[SUBLANES * r:SUBLANES * (r + 1)] >= cand, a + 1.0, a)
            return tuple(accs)

        accs = lax.fori_loop(0, nblk, body, tuple(jnp.zeros((SUBLANES, qb), F32) for _ in range(4)))
        return jnp.sum(accs[0] + accs[1] + accs[2] + accs[3], axis=0, keepdims=True)

    kf = float(topk)
    def step(c, settled, trial, cnt):
        c = jnp.where(jnp.logical_and(cnt >= kf, settled == 0.0), trial, c)
        return c, jnp.where(cnt == kf, 1.0, settled)

    c, settled = step(jnp.full((1, qb), INT32_MIN, jnp.int32), jnp.zeros((1, qb), F32),
                      jnp.zeros((1, qb), jnp.int32), count_ge_high(jnp.zeros((1, qb), BF16)))

    def high_bit_body(i, state):
        c, settled = state
        trial = c + jnp.left_shift(jnp.int32(1), 30 - i)
        return step(c, settled, trial, count_ge_high(_high_half(_key_to_f32(trial))))

    c, settled = lax.fori_loop(0, 15, high_bit_body, (c, settled))

    def low_bit_step(i, c, settled):
        trial = c + jnp.left_shift(jnp.int32(1), 15 - i)
        return step(c, settled, trial, count_ge(_key_to_f32(trial)))

    c, settled = lax.fori_loop(
        0, LOW_BITS_UNCHECKED, lambda i, st: low_bit_step(i, *st), (c, settled))

    def low_bit_cond(state):
        i, _, _, unsettled = state
        return jnp.logical_and(i < 16, unsettled > 0.0)

    def low_bit_body(state):
        i, c, settled, _ = state
        c, settled = low_bit_step(i, c, settled)
        return i + 1, c, settled, jnp.max(1.0 - settled)

    _, c, settled, unsettled = lax.while_loop(
        low_bit_cond, low_bit_body,
        (jnp.int32(LOW_BITS_UNCHECKED), c, settled, jnp.max(1.0 - settled)))
    thr = jnp.where(c == INT32_MIN, -3.0e38, _key_to_f32(c))

    tied = jnp.logical_and(settled == 0.0, c != INT32_MIN)

    @pl.when(unsettled > 0.0)
    def _():
        def count_rows(pred):
            def body(j, acc):
                s0 = pl.multiple_of(j * kc, kc)
                m = jnp.where(pred(s_ref[pl.ds(s0, kc), :], row_i + s0), 1.0, 0.0)
                return acc + jnp.sum(m, axis=0, keepdims=True)

            return lax.fori_loop(0, nblk, body, jnp.zeros((1, qb), F32))

        need = kf - count_rows(lambda blk, rows: blk > thr)

        def row_bit_body(i, r):
            trial = r + jnp.left_shift(jnp.int32(1), row_bits - 1 - i)
            before = count_rows(lambda blk, rows: jnp.logical_and(blk == thr, rows < trial))
            return jnp.where(before < need, trial, r)

        last = lax.fori_loop(0, row_bits, row_bit_body, jnp.zeros((1, qb), jnp.int32))

        def demote_body(j, carry):
            s0 = pl.multiple_of(j * kc, kc)
            blk = s_ref[pl.ds(s0, kc), :]
            drop = jnp.logical_and(jnp.logical_and(blk == thr, row_i + s0 > last), tied)
            s_ref[pl.ds(s0, kc), :] = jnp.where(drop, -jnp.inf, blk)
            return carry

        lax.fori_loop(0, nblk, demote_body, 0)

    acc_ref[...] = jnp.zeros(acc_ref.shape, F32)

    def logits(jn):
        s0 = pl.multiple_of(jn * kc, kc)
        return [_bdot(k_ref[0, pl.ds(s0, kc), h * HEAD_DIM:(h + 1) * HEAD_DIM],
                      qT_ref[0, h * HEAD_DIM:(h + 1) * HEAD_DIM, :]) for h in range(A_HEADS)]

    def stage_logits(lgs):
        for h in range(A_HEADS):
            lg_ref[h * kc:(h + 1) * kc, :] = lgs[h]

    stage_logits(logits(0))

    def attn_body(j, ms):
        next_lgs = logits(jnp.minimum(j + 1, nblk - 1))
        s0 = pl.multiple_of(j * kc, kc)
        bias = jnp.where(s_ref[pl.ds(s0, kc), :] >= thr, 0.0, MASKED_LOGIT)
        new_ms = []
        for h in range(A_HEADS):
            vr = slice(h * V_ROWS, (h + 1) * V_ROWS)
            lg = lg_ref[h * kc:(h + 1) * kc, :] + bias
            m_new = jnp.maximum(ms[h], jnp.max(lg, axis=0, keepdims=True))
            p = jnp.exp2(lg - m_new).astype(BF16)
            acc_ref[vr, :] = jnp.exp2(ms[h] - m_new) * acc_ref[vr, :] + _bdot(vT_ref[0, j, vr, :], p)
            new_ms.append(m_new)
        stage_logits(next_lgs)
        return tuple(new_ms)

    lax.fori_loop(0, nblk, attn_body,
                  tuple(jnp.full((1, qb), MASKED_LOGIT, F32) for _ in range(A_HEADS)))
    outs = []
    for h in range(A_HEADS):
        denom = acc_ref[h * V_ROWS + HEAD_DIM:h * V_ROWS + HEAD_DIM + 1, :]
        outs.append(acc_ref[h * V_ROWS:h * V_ROWS + HEAD_DIM, :] * (1.0 / denom))
    o_ref[0] = jnp.concatenate(outs, axis=0).T.astype(BF16)


def _dsa(qT, k, vT, iqT, ik, iwT, *, B, T):
    qb, kc = QUERY_BLOCK, KEY_CHUNK
    aw = A_HEADS * HEAD_DIM
    iqw = IDX_HEADS * IDX_DIM
    topk = min(TOPK_MAX, T // 4)
    blk = lambda b, q: (b, 0, q)
    per_batch = lambda shape: pl.BlockSpec(
        (1,) + shape, lambda b, q: (b,) + (0,) * len(shape), pipeline_mode=pl.Buffered(1))
    return pl.pallas_call(
        functools.partial(_dsa_kernel, topk=topk),
        grid=(B, T // qb),
        in_specs=[
            pl.BlockSpec((1, iqw, qb), blk),
            pl.BlockSpec((1, IDX_HEADS, qb), blk),
            pl.BlockSpec((1, aw, qb), blk),
            per_batch((T, IDX_DIM)),
            per_batch((T, aw)),
            per_batch((T // kc, A_HEADS * V_ROWS, kc)),
        ],
        out_specs=pl.BlockSpec((1, qb, aw), lambda b, q: (b, q, 0)),
        out_shape=jax.ShapeDtypeStruct((B, T, aw), BF16),
        scratch_shapes=[
            pltpu.VMEM((T, qb), F32),
            pltpu.VMEM((T, qb), BF16),
            pltpu.VMEM((A_HEADS * V_ROWS, qb), F32),
            pltpu.VMEM((A_HEADS * kc, qb), F32),
        ],
        compiler_params=_params(2),
        name="dsa",
    )(iqT, iwT, qT, ik, k, vT)


def _mem_kernel(mem_ref, g_ref, wk_ref, wv_ref, gk_ref, kT_ref, v_ref):
    dh = gk_ref.shape[-1]
    mn = _rms_rows(mem_ref[0], g_ref[0]).astype(BF16)
    k = _bdot(mn, wk_ref[0])
    v_ref[0, 0] = _bdot(mn, wv_ref[0]).astype(BF16)
    for c in range(C_HEADS):
        kc = _rms_rows(k[:, c * dh:(c + 1) * dh], gk_ref[0])
        kT_ref[0, 0, c] = kc.T.astype(BF16)


def _mem_kv(mem, g, wk, wv, gk):
    B, M, D = mem.shape
    L = wk.shape[0]
    dh = D // C_HEADS
    lay = lambda *tail: pl.BlockSpec((1,) + tail, lambda l, b: (l,) + (0,) * len(tail))
    return pl.pallas_call(
        _mem_kernel,
        grid=(L, B),
        in_specs=[
            pl.BlockSpec((1, M, D), lambda l, b: (b, 0, 0)),
            lay(1, D), lay(D, D), lay(D, D), lay(1, dh),
        ],
        out_specs=(
            pl.BlockSpec((1, 1, C_HEADS, dh, M), lambda l, b: (l, b, 0, 0, 0)),
            pl.BlockSpec((1, 1, M, D), lambda l, b: (l, b, 0, 0)),
        ),
        out_shape=(
            jax.ShapeDtypeStruct((L, B, C_HEADS, dh, M), BF16),
            jax.ShapeDtypeStruct((L, B, M, D), BF16),
        ),
        compiler_params=_params(2),
        name="mem_kv",
    )(mem, g, wk, wv, gk)


def _cross_kernel(*refs, mix):
    if mix:
        x_ref, a_ref, b_ref, wa_ref, wb_ref = refs[:5]
        h = x_ref[...] + _bdot(a_ref[...], wa_ref[...]) + _bdot(b_ref[...], wb_ref[...])
    else:
        h = refs[0][...]
    g_ref, wq_ref, gq_ref, kT_ref, v_ref, wo_ref, o_ref = refs[-7:]
    dh = gq_ref.shape[-1]
    q = _bdot(_rms_rows(h, g_ref[...]).astype(BF16), wq_ref[...])
    heads = []
    for c in range(C_HEADS):
        cs = slice(c * dh, (c + 1) * dh)
        qc = _rms_rows(q[:, cs], gq_ref[...]).astype(BF16)
        lg = _bdot(qc, kT_ref[0, c]) * dh ** -0.5
        p = jnp.exp(lg - jnp.max(lg, axis=-1, keepdims=True))
        p = p * (1.0 / jnp.sum(p, axis=-1, keepdims=True))
        heads.append(_bdot(p.astype(BF16), v_ref[0, :, cs]))
    o = jnp.concatenate(heads, axis=-1).astype(BF16)
    o_ref[...] = h + _bdot(o, wo_ref[...])


def _cross(stream, mix_weights, g, wq, gq, kT, v, wo, *, T):
    N, D = stream[0].shape
    tm = ROW_TILE
    tps = T // tm
    dh = D // C_HEADS
    M = v.shape[1]
    return pl.pallas_call(
        functools.partial(_cross_kernel, mix=bool(mix_weights)),
        grid=(N // tm,),
        in_specs=[pl.BlockSpec((tm, s.shape[1]), lambda i: (i, 0)) for s in stream]
        + [_resident(w.shape) for w in mix_weights] + [
            _resident(g.shape), _resident(wq.shape), _resident(gq.shape),
            pl.BlockSpec((1, C_HEADS, dh, M), lambda i: (i // tps, 0, 0, 0)),
            pl.BlockSpec((1, M, D), lambda i: (i // tps, 0, 0)),
            _resident(wo.shape),
        ],
        out_specs=pl.BlockSpec((tm, D), lambda i: (i, 0)),
        out_shape=jax.ShapeDtypeStruct((N, D), F32),
        compiler_params=_params(1),
        name="cross",
    )(*stream, *mix_weights, g, wq, gq, kT, v, wo)


def _ffn_kernel(h_ref, g_ref, wg_ref, wu_ref, wd_ref, o_ref, *, n_chunks):
    h = h_ref[...]
    hn = _rms_rows(h, g_ref[...]).astype(BF16)
    fc = wg_ref.shape[1] // n_chunks
    acc = h
    for c in range(n_chunks):
        cs = slice(c * fc, (c + 1) * fc)
        gate = _bdot(hn, wg_ref[:, cs])
        act = gate * jax.nn.sigmoid(gate) * _bdot(hn, wu_ref[:, cs])
        acc = acc + _bdot(act.astype(BF16), wd_ref[cs, :])
    o_ref[...] = acc


def _ffn(h2, g, wg, wu, wd):
    N, D = h2.shape
    tm = ROW_TILE
    return pl.pallas_call(
        functools.partial(_ffn_kernel, n_chunks=2),
        grid=(N // tm,),
        in_specs=[
            pl.BlockSpec((tm, D), lambda i: (i, 0)),
            _resident(g.shape), _resident(wg.shape), _resident(wu.shape), _resident(wd.shape),
        ],
        out_specs=pl.BlockSpec((tm, D), lambda i: (i, 0)),
        out_shape=jax.ShapeDtypeStruct((N, D), F32),
        compiler_params=_params(1),
        name="ffn",
    )(h2, g, wg, wu, wd)


def _conv_kernel(h_ref, hprev_ref, g_ref, win_ref, bin_ref, dw_ref, dwb_ref, lng_ref, lnb_ref,
                 wout_ref, o_ref, cbuf_ref, shift_ref, gnext_ref, *, tiles_per_seq):
    tm, D = h_ref.shape
    i = pl.program_id(0)

    @pl.when(i == 0)
    def _():
        cbuf_ref[...] = jnp.zeros(cbuf_ref.shape, F32)

    @pl.when((i + tiles_per_seq - 1) % tiles_per_seq == 0)
    def _():
        cbuf_ref[0:CONV_HIST, :] = jnp.zeros((CONV_HIST, D), F32)

    hh = _bdot(_rms_rows(h_ref[...], g_ref[...]).astype(BF16), win_ref[...]) + bin_ref[...]
    gnext_ref[...] = hh[:, :D] * jax.nn.sigmoid(hh[:, D:])

    off = CONV_HIST - (CONV_WIDTH - 1)
    span = tm + CONV_HIST - SUBLANES
    for r in range(1, SUBLANES):
        shift_ref[r - 1] = cbuf_ref[r:r + span, :]
    blocks = []
    for cb in range(D // LANES):
        cs = slice(cb * LANES, (cb + 1) * LANES)
        acc = None
        for j in range(CONV_WIDTH):
            a, r = divmod(off + j, SUBLANES)
            src = cbuf_ref if r == 0 else shift_ref.at[r - 1]
            term = src[a * SUBLANES:a * SUBLANES + tm, cs] * dw_ref[j:j + 1, cs]
            acc = term if acc is None else acc + term
        blocks.append(acc)

    y = jnp.concatenate(blocks, axis=-1) + dwb_ref[...]
    mu = jnp.mean(y, axis=-1, keepdims=True)
    yc = y - mu
    var = jnp.mean(yc * yc, axis=-1, keepdims=True)
    yn = yc * lax.rsqrt(var + EPS) * lng_ref[...] + lnb_ref[...]
    act = yn * jax.nn.sigmoid(yn)
    o_ref[...] = hprev_ref[...] + _bdot(act.astype(BF16), wout_ref[...])

    cbuf_ref[0:CONV_HIST, :] = cbuf_ref[tm:tm + CONV_HIST, :]
    cbuf_ref[CONV_HIST:CONV_HIST + tm, :] = gnext_ref[...]


def _conv(h2, g, win, bin_, dw, dwb, lng, lnb, wout, *, T):
    N, D = h2.shape
    tm = CONV_TILE
    tps = T // tm
    consts = (g, win, bin_, dw, dwb, lng, lnb, wout)
    n_tiles = N // tm
    prev_tile = lambda i: (jnp.maximum(i - 1, 0), 0)
    return pl.pallas_call(
        functools.partial(_conv_kernel, tiles_per_seq=tps),
        grid=(n_tiles + 1,),
        in_specs=[pl.BlockSpec((tm, D), lambda i: (jnp.minimum(i, n_tiles - 1), 0)),
                  pl.BlockSpec((tm, D), prev_tile)] + [_resident(c.shape) for c in consts],
        out_specs=pl.BlockSpec((tm, D), prev_tile),
        out_shape=jax.ShapeDtypeStruct((N, D), F32),
        scratch_shapes=[
            pltpu.VMEM((CONV_HIST + tm, D), F32),
            pltpu.VMEM((SUBLANES - 1, tm + CONV_HIST - SUBLANES, D), F32),
            pltpu.VMEM((tm, D), F32),
        ],
        compiler_params=_params(1),
        name="conv",
    )(h2, h2, *consts)


def _rope_tables(T, rot):
    half = rot // 2
    inv = ROPE_THETA ** (-jnp.arange(half, dtype=F32) * 2.0 / rot)
    ang = jnp.arange(T).astype(F32)[:, None] * inv[None, :]
    return jnp.cos(ang).T, jnp.sin(ang).T


def kernel(x, mem, norm_mix, norm_cross, norm_mem, norm_ffn, w_in_ab, a_q_norm, a_k_norm, pool_w, pool_scale, w_out_ab, conv_w_in, conv_b_in, conv_dw_w, conv_dw_b, conv_ln_g, conv_ln_b, conv_w_out, cross_wq, cross_wk, cross_wv, cross_q_norm, cross_k_norm, cross_wo, ffn_w_gate, ffn_w_up, ffn_w_down):
    B, T, D = x.shape
    N = B * T
    depth = norm_mix.shape[0]
    aw = A_HEADS * HEAD_DIM
    iqw = IDX_HEADS * IDX_DIM
    pw = len(POOL_WINDOWS) * POOL_GROUP
    assert T % ROW_TILE == 0 and T % KEY_CHUNK == 0 and KEY_CHUNK % QUERY_BLOCK == 0
    assert T % CONV_TILE == 0 and CONV_TILE >= CONV_HIST
    row = lambda v: v.reshape(1, -1)

    kT_mem, v_mem = _mem_kv(mem, norm_mem[:, None, :], cross_wk.astype(BF16),
                            cross_wv.astype(BF16), cross_k_norm[:, None, :])
    cos16, sin16 = _rope_tables(T, HEAD_DIM // 4)
    cos8, sin8 = _rope_tables(T, IDX_DIM // 4)

    h = x.reshape(N, D)
    for l in range(depth):
        if l % 2 == 0:
            e = l // 2
            w_in = w_in_ab[e]
            o_u, o_iq, o_iw, o_ik = 3 * aw, 3 * aw + pw, 3 * aw + pw + iqw, 3 * aw + pw + iqw + IDX_HEADS
            wt = jnp.concatenate(
                [w_in[:, :3 * aw], w_in[:, o_iq:o_iw], w_in[:, o_ik:], w_in[:, o_iw:o_ik]],
                axis=1).T.astype(BF16)
            wu = w_in[:, o_u:o_iq].astype(BF16)
            qT, k, vT, iqT, ik, iwT, b2 = _front(
                h, row(norm_mix[l]), wt, wu, a_q_norm[e][:, None], a_k_norm[e][:, None],
                cos16, sin16, cos8, sin8, pool_w[e].astype(BF16), row(pool_scale[e]), B=B, T=T)
            a = _dsa(qT, k, vT, iqT, ik, iwT, B=B, T=T)
            w_out = w_out_ab[e].astype(BF16)
            stream, mix_weights = (h, a.reshape(N, aw), b2), (w_out[:aw], w_out[aw:])
        else:
            o = l // 2
            h = _conv(h, row(norm_mix[l]), conv_w_in[o].astype(BF16), row(conv_b_in[o]),
                      conv_dw_w[o], row(conv_dw_b[o]), row(conv_ln_g[o]), row(conv_ln_b[o]),
                      conv_w_out[o].astype(BF16), T=T)
            stream, mix_weights = (h,), ()
        h = _cross(stream, mix_weights, row(norm_cross[l]), cross_wq[l].astype(BF16),
                   row(cross_q_norm[l]), kT_mem[l], v_mem[l], cross_wo[l].astype(BF16), T=T)
        h = _ffn(h, row(norm_ffn[l]), ffn_w_gate[l].astype(BF16), ffn_w_up[l].astype(BF16),
                 ffn_w_down[l].astype(BF16))
    return h.reshape(B, T, D)
```

```python
import functools

import jax
import jax.numpy as jnp
from jax import lax
from jax.experimental import pallas as pl
from jax.experimental.pallas import tpu as pltpu

F32 = jnp.float32
BF16 = jnp.bfloat16

EPS = 1e-6
A_HEADS = 4
HEAD_DIM = 128
IDX_HEADS = 16
IDX_DIM = 64
TOPK_MAX = 256
ROPE_THETA = 500000.0
POOL_WINDOWS = (2, 4, 8, 16)
POOL_GROUP = 128
POOL_HIST = 16
C_HEADS = 4
CONV_WIDTH = 31
CONV_HIST = 32
SUBLANES = 8
LANES = 128
BF16_ROWS = 16

VMEM_LIMIT_BYTES = 56 * 1024 * 1024
QUERY_BLOCK = 256
KEY_CHUNK = 512
CONV_TILE = 256
ROW_TILE = 512
FFN_CHUNKS = 2
INT32_MIN = -2 ** 31
MASKED_LOGIT = -1e30
V_PAD = 16
V_ROWS = HEAD_DIM + V_PAD
LOG2_E = 1.4426950408889634
KEY_BITS = 32
LOW_KEY_BITS = 16
LOW_BITS_UNCHECKED = 8
INT32_MAX = 2 ** 31 - 1
HIGH_HALF_MASK = -(1 << LOW_KEY_BITS)
SELECT_ALL_THRESHOLD = -3.0e38


def _params(n_axes):
    return pltpu.CompilerParams(
        dimension_semantics=("arbitrary",) * n_axes,
        vmem_limit_bytes=VMEM_LIMIT_BYTES)


def _resident(shape):
    zeros = (0,) * len(shape)
    return pl.BlockSpec(shape, lambda *_: zeros, pipeline_mode=pl.Buffered(1))


def _rms_rows(x, g):
    ms = jnp.mean(x * x, axis=-1, keepdims=True)
    return x * lax.rsqrt(ms + EPS) * g


def _bdot(a, b):
    return jnp.dot(a, b, preferred_element_type=F32)


def _rope_cols(y, cos, sin, half):
    y1, y2 = y[0:half], y[half:2 * half]
    return jnp.concatenate(
        [y1 * cos - y2 * sin, y2 * cos + y1 * sin, y[2 * half:]], axis=0)


def _front_kernel(x_ref, g_ref, wt_ref, wu_ref, gq_ref, gk_ref, cos16_ref, sin16_ref,
                  cos8_ref, sin8_ref, poolw_ref, pscale_ref,
                  qT_ref, k_ref, vT_ref, iqT_ref, ik_ref, iwT_ref, b_ref, ubuf_ref,
                  *, tiles_per_seq):
    tm = x_ref.shape[0]
    i = pl.program_id(0)
    aw = A_HEADS * HEAD_DIM
    iqw = IDX_HEADS * IDX_DIM

    xn = _rms_rows(x_ref[...], g_ref[...])
    xb = xn.astype(BF16)
    xT = xn.T.astype(BF16)
    cols = _bdot(wt_ref[...], xT)

    cos16, sin16 = cos16_ref[...], sin16_ref[...]
    cos8, sin8 = cos8_ref[...], sin8_ref[...]

    def qk_heads(base, gain):
        heads = []
        for h in range(A_HEADS):
            xh = cols[base + h * HEAD_DIM: base + (h + 1) * HEAD_DIM]
            ms = jnp.sum(xh * xh, axis=0, keepdims=True) * (1.0 / HEAD_DIM)
            y = xh * lax.rsqrt(ms + EPS) * gain
            heads.append(_rope_cols(y, cos16, sin16, HEAD_DIM // 8))
        return jnp.concatenate(heads, axis=0)

    qT_ref[0] = (qk_heads(0, gq_ref[...]) * (HEAD_DIM ** -0.5 * LOG2_E)).astype(BF16)
    k_ref[0] = qk_heads(aw, gk_ref[...]).T.astype(BF16)
    ones_blk = (lax.broadcasted_iota(jnp.int32, (V_PAD, tm), 0) == 0).astype(F32)
    v_rows = []
    for h in range(A_HEADS):
        v_rows += [cols[2 * aw + h * HEAD_DIM:2 * aw + (h + 1) * HEAD_DIM], ones_blk]
    vT_ref[0, 0] = jnp.concatenate(v_rows, axis=0).astype(BF16)

    base = 3 * aw
    iq_heads = [
        _rope_cols(cols[base + h * IDX_DIM: base + (h + 1) * IDX_DIM], cos8, sin8, IDX_DIM // 8)
        for h in range(IDX_HEADS)]
    iqT_ref[0] = jnp.concatenate(iq_heads, axis=0).astype(BF16)

    base = 3 * aw + iqw
    ikT = _rope_cols(cols[base:base + IDX_DIM], cos8, sin8, IDX_DIM // 8)
    ikT = jnp.concatenate([ikT, jnp.zeros_like(ikT)], axis=0)
    ik_ref[0] = ikT.T[:, :IDX_DIM].astype(BF16)

    base = 3 * aw + iqw + IDX_DIM
    iwT_ref[0] = cols[base:base + IDX_HEADS] * (IDX_HEADS ** -0.5 * IDX_DIM ** -0.5)

    u = _bdot(xb, wu_ref[...])
    t_in_seq = (i % tiles_per_seq) * tm

    @pl.when(i % tiles_per_seq == 0)
    def _():
        ubuf_ref[0:POOL_HIST, :] = jnp.zeros((POOL_HIST, u.shape[1]), F32)

    ubuf_ref[POOL_HIST:POOL_HIST + tm, :] = u
    t1 = (lax.broadcasted_iota(jnp.int32, (tm, 1), 0) + t_in_seq + 1).astype(F32)
    outs = []
    for g, w in enumerate(POOL_WINDOWS):
        cs = slice(g * POOL_GROUP, (g + 1) * POOL_GROUP)
        acc = u[:, cs]
        for j in range(1, w):
            acc = acc + ubuf_ref[POOL_HIST - j:POOL_HIST - j + tm, cs]
        pooled = acc / jnp.minimum(t1, float(w)) - u[:, cs]
        outs.append(_bdot(pooled.astype(BF16), poolw_ref[g]))
    b_ref[...] = (jnp.concatenate(outs, axis=-1) * pscale_ref[...]).astype(BF16)
    ubuf_ref[0:POOL_HIST, :] = u[tm - POOL_HIST:tm, :]


def _front(x2, g, wt, wu, gq, gk, cos16, sin16, cos8, sin8, poolw, pscale, *, B, T):
    N, D = x2.shape
    tm = KEY_CHUNK
    tps = T // tm
    aw = A_HEADS * HEAD_DIM
    iqw = IDX_HEADS * IDX_DIM
    pw = len(POOL_WINDOWS) * POOL_GROUP
    seq_map = lambda i: (i // tps, 0, i % tps)
    out_shape = (
        jax.ShapeDtypeStruct((B, aw, T), BF16),
        jax.ShapeDtypeStruct((B, T, aw), BF16),
        jax.ShapeDtypeStruct((B, tps, A_HEADS * V_ROWS, tm), BF16),
        jax.ShapeDtypeStruct((B, iqw, T), BF16),
        jax.ShapeDtypeStruct((B, T, IDX_DIM), BF16),
        jax.ShapeDtypeStruct((B, IDX_HEADS, T), F32),
        jax.ShapeDtypeStruct((N, pw), BF16),
    )
    return pl.pallas_call(
        functools.partial(_front_kernel, tiles_per_seq=tps),
        grid=(N // tm,),
        in_specs=[
            pl.BlockSpec((tm, D), lambda i: (i, 0)),
            _resident(g.shape), _resident(wt.shape), _resident(wu.shape),
            _resident(gq.shape), _resident(gk.shape),
            pl.BlockSpec((HEAD_DIM // 8, tm), lambda i: (0, i % tps)),
            pl.BlockSpec((HEAD_DIM // 8, tm), lambda i: (0, i % tps)),
            pl.BlockSpec((IDX_DIM // 8, tm), lambda i: (0, i % tps)),
            pl.BlockSpec((IDX_DIM // 8, tm), lambda i: (0, i % tps)),
            _resident(poolw.shape), _resident(pscale.shape),
        ],
        out_specs=(
            pl.BlockSpec((1, aw, tm), seq_map),
            pl.BlockSpec((1, tm, aw), lambda i: (i // tps, i % tps, 0)),
            pl.BlockSpec((1, 1, A_HEADS * V_ROWS, tm), lambda i: (i // tps, i % tps, 0, 0)),
            pl.BlockSpec((1, iqw, tm), seq_map),
            pl.BlockSpec((1, tm, IDX_DIM), lambda i: (i // tps, i % tps, 0)),
            pl.BlockSpec((1, IDX_HEADS, tm), seq_map),
            pl.BlockSpec((tm, pw), lambda i: (i, 0)),
        ),
        out_shape=out_shape,
        scratch_shapes=[pltpu.VMEM((POOL_HIST + tm, pw), F32)],
        compiler_params=_params(1),
        name="front",
    )(x2, g, wt, wu, gq, gk, cos16, sin16, cos8, sin8, poolw, pscale)


def _key_to_f32(c):
    bits = c ^ ((c >> (KEY_BITS - 1)) & INT32_MAX)
    return lax.bitcast_convert_type(bits, F32)


def _high_half(x):
    bits = lax.bitcast_convert_type(x, jnp.int32) & jnp.int32(HIGH_HALF_MASK)
    return lax.bitcast_convert_type(bits, F32).astype(BF16)


def _dsa_kernel(iqT_ref, iwT_ref, qT_ref, ik_ref, k_ref, vT_ref, o_ref,
                s_ref, s16_ref, acc_ref, lg_ref, *, topk):
    qb, kc = QUERY_BLOCK, KEY_CHUNK
    row_bits = (s_ref.shape[0] - 1).bit_length()
    qi = pl.program_id(1)
    nblk = ((qi + 1) * qb + kc - 1) // kc

    row_i = lax.broadcasted_iota(jnp.int32, (kc, qb), 0)
    col_i = lax.broadcasted_iota(jnp.int32, (kc, qb), 1)

    def score_body(j, carry):
        s0 = pl.multiple_of(j * kc, kc)
        lhs = ik_ref[0, pl.ds(s0, kc), :]
        for h in range(IDX_HEADS):
            rhs = iqT_ref[0, h * IDX_DIM:(h + 1) * IDX_DIM, :]
            val = jnp.maximum(_bdot(lhs, rhs), 0.0) * iwT_ref[0, h:h + 1, :]
            if h > 0:
                val = s_ref[pl.ds(s0, kc), :] + val
            if h == IDX_HEADS - 1:
                val = jnp.where(row_i + s0 <= col_i + qi * qb, val, -jnp.inf)
                s16_ref[pl.ds(s0, kc), :] = _high_half(val)
            s_ref[pl.ds(s0, kc), :] = val
        return carry

    lax.fori_loop(0, nblk, score_body, 0)

    def count_ge_high(cand16):
        def body(j, acc):
            s0 = pl.multiple_of(j * kc, kc)
            blk = s16_ref[pl.ds(s0, kc), :]
            one = jnp.ones((), BF16)
            parts = [jnp.zeros((BF16_ROWS, qb), BF16) for _ in range(4)]
            for r in range(kc // BF16_ROWS):
                a = parts[r % 4]
                parts[r % 4] = jnp.where(blk[BF16_ROWS * r:BF16_ROWS * (r + 1)] >= cand16, a + one, a)
            part = (parts[0] + parts[1]) + (parts[2] + parts[3])
            return acc + part.astype(F32)

        acc = lax.fori_loop(0, nblk, body, jnp.zeros((BF16_ROWS, qb), F32))
        return jnp.sum(acc, axis=0, keepdims=True)

    def count_ge(cand):
        def body(j, accs):
            s0 = pl.multiple_of(j * kc, kc)
            blk = s_ref[pl.ds(s0, kc), :]
            accs = list(accs)
            for r in range(kc // SUBLANES):
                a = accs[r % len(accs)]
                accs[r % len(accs)] = jnp.where(blk[SUBLANES * r:SUBLANES * (r + 1)] >= cand, a + 1.0, a)
            return tuple(accs)

        accs = lax.fori_loop(0, nblk, body, tuple(jnp.zeros((SUBLANES, qb), F32) for _ in range(4)))
        return jnp.sum(accs[0] + accs[1] + accs[2] + accs[3], axis=0, keepdims=True)

    kf = float(topk)
    def step(c, settled, trial, cnt):
        c = jnp.where(jnp.logical_and(cnt >= kf, settled == 0.0), trial, c)
        return c, jnp.where(cnt == kf, 1.0, settled)

    c, settled = step(jnp.full((1, qb), INT32_MIN, jnp.int32), jnp.zeros((1, qb), F32),
                      jnp.zeros((1, qb), jnp.int32), count_ge_high(jnp.zeros((1, qb), BF16)))

    def high_bit_body(i, state):
        c, settled = state
        trial = c + jnp.left_shift(jnp.int32(1), KEY_BITS - 2 - i)
        return step(c, settled, trial, count_ge_high(_high_half(_key_to_f32(trial))))

    c, settled = lax.fori_loop(0, KEY_BITS - 1 - LOW_KEY_BITS, high_bit_body, (c, settled))

    def low_bit_step(i, c, settled):
        trial = c + jnp.left_shift(jnp.int32(1), LOW_KEY_BITS - 1 - i)
        return step(c, settled, trial, count_ge(_key_to_f32(trial)))

    c, settled = lax.fori_loop(
        0, LOW_BITS_UNCHECKED, lambda i, st: low_bit_step(i, *st), (c, settled))

    def low_bit_cond(state):
        i, _, _, unsettled = state
        return jnp.logical_and(i < LOW_KEY_BITS, unsettled > 0.0)

    def low_bit_body(state):
        i, c, settled, _ = state
        c, settled = low_bit_step(i, c, settled)
        return i + 1, c, settled, jnp.max(1.0 - settled)

    _, c, settled, unsettled = lax.while_loop(
        low_bit_cond, low_bit_body,
        (jnp.int32(LOW_BITS_UNCHECKED), c, settled, jnp.max(1.0 - settled)))
    thr = jnp.where(c == INT32_MIN, SELECT_ALL_THRESHOLD, _key_to_f32(c))

    tied = jnp.logical_and(settled == 0.0, c != INT32_MIN)

    @pl.when(unsettled > 0.0)
    def _():
        def count_rows(pred):
            def body(j, acc):
                s0 = pl.multiple_of(j * kc, kc)
                m = jnp.where(pred(s_ref[pl.ds(s0, kc), :], row_i + s0), 1.0, 0.0)
                return acc + jnp.sum(m, axis=0, keepdims=True)

            return lax.fori_loop(0, nblk, body, jnp.zeros((1, qb), F32))

        need = kf - count_rows(lambda blk, rows: blk > thr)

        def row_bit_body(i, r):
            trial = r + jnp.left_shift(jnp.int32(1), row_bits - 1 - i)
            before = count_rows(lambda blk, rows: jnp.logical_and(blk == thr, rows < trial))
            return jnp.where(before < need, trial, r)

        last = lax.fori_loop(0, row_bits, row_bit_body, jnp.zeros((1, qb), jnp.int32))

        def demote_body(j, carry):
            s0 = pl.multiple_of(j * kc, kc)
            blk = s_ref[pl.ds(s0, kc), :]
            drop = jnp.logical_and(jnp.logical_and(blk == thr, row_i + s0 > last), tied)
            s_ref[pl.ds(s0, kc), :] = jnp.where(drop, -jnp.inf, blk)
            return carry

        lax.fori_loop(0, nblk, demote_body, 0)

    acc_ref[...] = jnp.zeros(acc_ref.shape, F32)

    def logits(jn):
        s0 = pl.multiple_of(jn * kc, kc)
        return [_bdot(k_ref[0, pl.ds(s0, kc), h * HEAD_DIM:(h + 1) * HEAD_DIM],
                      qT_ref[0, h * HEAD_DIM:(h + 1) * HEAD_DIM, :]) for h in range(A_HEADS)]

    def stage_logits(lgs):
        for h in range(A_HEADS):
            lg_ref[h * kc:(h + 1) * kc, :] = lgs[h]

    stage_logits(logits(0))

    def attn_body(j, ms):
        next_lgs = logits(jnp.minimum(j + 1, nblk - 1))
        s0 = pl.multiple_of(j * kc, kc)
        bias = jnp.where(s_ref[pl.ds(s0, kc), :] >= thr, 0.0, MASKED_LOGIT)
        new_ms = []
        for h in range(A_HEADS):
            vr = slice(h * V_ROWS, (h + 1) * V_ROWS)
            lg = lg_ref[h * kc:(h + 1) * kc, :] + bias
            m_new = jnp.maximum(ms[h], jnp.max(lg, axis=0, keepdims=True))
            p = jnp.exp2(lg - m_new).astype(BF16)
            acc_ref[vr, :] = jnp.exp2(ms[h] - m_new) * acc_ref[vr, :] + _bdot(vT_ref[0, j, vr, :], p)
            new_ms.append(m_new)
        stage_logits(next_lgs)
        return tuple(new_ms)

    lax.fori_loop(0, nblk, attn_body,
                  tuple(jnp.full((1, qb), MASKED_LOGIT, F32) for _ in range(A_HEADS)))
    outs = []
    for h in range(A_HEADS):
        denom = acc_ref[h * V_ROWS + HEAD_DIM:h * V_ROWS + HEAD_DIM + 1, :]
        outs.append(acc_ref[h * V_ROWS:h * V_ROWS + HEAD_DIM, :] * (1.0 / denom))
    o_ref[0] = jnp.concatenate(outs, axis=0).T.astype(BF16)


def _dsa(qT, k, vT, iqT, ik, iwT, *, B, T):
    qb, kc = QUERY_BLOCK, KEY_CHUNK
    aw = A_HEADS * HEAD_DIM
    iqw = IDX_HEADS * IDX_DIM
    topk = min(TOPK_MAX, T // 4)
    blk = lambda b, q: (b, 0, q)
    per_batch = lambda shape: pl.BlockSpec(
        (1,) + shape, lambda b, q: (b,) + (0,) * len(shape), pipeline_mode=pl.Buffered(1))
    return pl.pallas_call(
        functools.partial(_dsa_kernel, topk=topk),
        grid=(B, T // qb),
        in_specs=[
            pl.BlockSpec((1, iqw, qb), blk),
            pl.BlockSpec((1, IDX_HEADS, qb), blk),
            pl.BlockSpec((1, aw, qb), blk),
            per_batch((T, IDX_DIM)),
            per_batch((T, aw)),
            per_batch((T // kc, A_HEADS * V_ROWS, kc)),
        ],
        out_specs=pl.BlockSpec((1, qb, aw), lambda b, q: (b, q, 0)),
        out_shape=jax.ShapeDtypeStruct((B, T, aw), BF16),
        scratch_shapes=[
            pltpu.VMEM((T, qb), F32),
            pltpu.VMEM((T, qb), BF16),
            pltpu.VMEM((A_HEADS * V_ROWS, qb), F32),
            pltpu.VMEM((A_HEADS * kc, qb), F32),
        ],
        compiler_params=_params(2),
        name="dsa",
    )(iqT, iwT, qT, ik, k, vT)


def _mem_kernel(mem_ref, g_ref, wk_ref, wv_ref, gk_ref, kT_ref, v_ref):
    dh = gk_ref.shape[-1]
    mn = _rms_rows(mem_ref[0], g_ref[0]).astype(BF16)
    k = _bdot(mn, wk_ref[0])
    v_ref[0, 0] = _bdot(mn, wv_ref[0]).astype(BF16)
    for c in range(C_HEADS):
        kc = _rms_rows(k[:, c * dh:(c + 1) * dh], gk_ref[0])
        kT_ref[0, 0, c] = kc.T.astype(BF16)


def _mem_kv(mem, g, wk, wv, gk):
    B, M, D = mem.shape
    L = wk.shape[0]
    dh = D // C_HEADS
    lay = lambda *tail: pl.BlockSpec((1,) + tail, lambda l, b: (l,) + (0,) * len(tail))
    return pl.pallas_call(
        _mem_kernel,
        grid=(L, B),
        in_specs=[
            pl.BlockSpec((1, M, D), lambda l, b: (b, 0, 0)),
            lay(1, D), lay(D, D), lay(D, D), lay(1, dh),
        ],
        out_specs=(
            pl.BlockSpec((1, 1, C_HEADS, dh, M), lambda l, b: (l, b, 0, 0, 0)),
            pl.BlockSpec((1, 1, M, D), lambda l, b: (l, b, 0, 0)),
        ),
        out_shape=(
            jax.ShapeDtypeStruct((L, B, C_HEADS, dh, M), BF16),
            jax.ShapeDtypeStruct((L, B, M, D), BF16),
        ),
        compiler_params=_params(2),
        name="mem_kv",
    )(mem, g, wk, wv, gk)


def _cross_kernel(*refs, mix):
    if mix:
        x_ref, a_ref, b_ref, wa_ref, wb_ref = refs[:5]
        h = x_ref[...] + _bdot(a_ref[...], wa_ref[...]) + _bdot(b_ref[...], wb_ref[...])
    else:
        h = refs[0][...]
    g_ref, wq_ref, gq_ref, kT_ref, v_ref, wo_ref, o_ref = refs[-7:]
    dh = gq_ref.shape[-1]
    q = _bdot(_rms_rows(h, g_ref[...]).astype(BF16), wq_ref[...])
    heads = []
    for c in range(C_HEADS):
        cs = slice(c * dh, (c + 1) * dh)
        qc = _rms_rows(q[:, cs], gq_ref[...]).astype(BF16)
        lg = _bdot(qc, kT_ref[0, c]) * dh ** -0.5
        p = jnp.exp(lg - jnp.max(lg, axis=-1, keepdims=True))
        p = p * (1.0 / jnp.sum(p, axis=-1, keepdims=True))
        heads.append(_bdot(p.astype(BF16), v_ref[0, :, cs]))
    o = jnp.concatenate(heads, axis=-1).astype(BF16)
    o_ref[...] = h + _bdot(o, wo_ref[...])


def _cross(stream, mix_weights, g, wq, gq, kT, v, wo, *, T):
    N, D = stream[0].shape
    tm = ROW_TILE
    tps = T // tm
    dh = D // C_HEADS
    M = v.shape[1]
    return pl.pallas_call(
        functools.partial(_cross_kernel, mix=bool(mix_weights)),
        grid=(N // tm,),
        in_specs=[pl.BlockSpec((tm, s.shape[1]), lambda i: (i, 0)) for s in stream]
        + [_resident(w.shape) for w in mix_weights] + [
            _resident(g.shape), _resident(wq.shape), _resident(gq.shape),
            pl.BlockSpec((1, C_HEADS, dh, M), lambda i: (i // tps, 0, 0, 0)),
            pl.BlockSpec((1, M, D), lambda i: (i // tps, 0, 0)),
            _resident(wo.shape),
        ],
        out_specs=pl.BlockSpec((tm, D), lambda i: (i, 0)),
        out_shape=jax.ShapeDtypeStruct((N, D), F32),
        compiler_params=_params(1),
        name="cross",
    )(*stream, *mix_weights, g, wq, gq, kT, v, wo)


def _ffn_kernel(h_ref, g_ref, wg_ref, wu_ref, wd_ref, o_ref, *, n_chunks):
    h = h_ref[...]
    hn = _rms_rows(h, g_ref[...]).astype(BF16)
    fc = wg_ref.shape[1] // n_chunks
    acc = h
    for c in range(n_chunks):
        cs = slice(c * fc, (c + 1) * fc)
        gate = _bdot(hn, wg_ref[:, cs])
        act = gate * jax.nn.sigmoid(gate) * _bdot(hn, wu_ref[:, cs])
        acc = acc + _bdot(act.astype(BF16), wd_ref[cs, :])
    o_ref[...] = acc


def _ffn(h2, g, wg, wu, wd):
    N, D = h2.shape
    tm = ROW_TILE
    return pl.pallas_call(
        functools.partial(_ffn_kernel, n_chunks=FFN_CHUNKS),
        grid=(N // tm,),
        in_specs=[
            pl.BlockSpec((tm, D), lambda i: (i, 0)),
            _resident(g.shape), _resident(wg.shape), _resident(wu.shape), _resident(wd.shape),
        ],
        out_specs=pl.BlockSpec((tm, D), lambda i: (i, 0)),
        out_shape=jax.ShapeDtypeStruct((N, D), F32),
        compiler_params=_params(1),
        name="ffn",
    )(h2, g, wg, wu, wd)


def _conv_kernel(h_ref, hprev_ref, g_ref, win_ref, bin_ref, dw_ref, dwb_ref, lng_ref, lnb_ref,
                 wout_ref, o_ref, cbuf_ref, shift_ref, gnext_ref, *, tiles_per_seq):
    tm, D = h_ref.shape
    i = pl.program_id(0)

    @pl.when(i == 0)
    def _():
        cbuf_ref[...] = jnp.zeros(cbuf_ref.shape, F32)

    @pl.when((i + tiles_per_seq - 1) % tiles_per_seq == 0)
    def _():
        cbuf_ref[0:CONV_HIST, :] = jnp.zeros((CONV_HIST, D), F32)

    hh = _bdot(_rms_rows(h_ref[...], g_ref[...]).astype(BF16), win_ref[...]) + bin_ref[...]
    gnext_ref[...] = hh[:, :D] * jax.nn.sigmoid(hh[:, D:])

    off = CONV_HIST - (CONV_WIDTH - 1)
    span = tm + CONV_HIST - SUBLANES
    for r in range(1, SUBLANES):
        shift_ref[r - 1] = cbuf_ref[r:r + span, :]
    blocks = []
    for cb in range(D // LANES):
        cs = slice(cb * LANES, (cb + 1) * LANES)
        acc = None
        for j in range(CONV_WIDTH):
            a, r = divmod(off + j, SUBLANES)
            src = cbuf_ref if r == 0 else shift_ref.at[r - 1]
            term = src[a * SUBLANES:a * SUBLANES + tm, cs] * dw_ref[j:j + 1, cs]
            acc = term if acc is None else acc + term
        blocks.append(acc)

    y = jnp.concatenate(blocks, axis=-1) + dwb_ref[...]
    mu = jnp.mean(y, axis=-1, keepdims=True)
    yc = y - mu
    var = jnp.mean(yc * yc, axis=-1, keepdims=True)
    yn = yc * lax.rsqrt(var + EPS) * lng_ref[...] + lnb_ref[...]
    act = yn * jax.nn.sigmoid(yn)
    o_ref[...] = hprev_ref[...] + _bdot(act.astype(BF16), wout_ref[...])

    cbuf_ref[0:CONV_HIST, :] = cbuf_ref[tm:tm + CONV_HIST, :]
    cbuf_ref[CONV_HIST:CONV_HIST + tm, :] = gnext_ref[...]


def _conv(h2, g, win, bin_, dw, dwb, lng, lnb, wout, *, T):
    N, D = h2.shape
    tm = CONV_TILE
    tps = T // tm
    consts = (g, win, bin_, dw, dwb, lng, lnb, wout)
    n_tiles = N // tm
    prev_tile = lambda i: (jnp.maximum(i - 1, 0), 0)
    return pl.pallas_call(
        functools.partial(_conv_kernel, tiles_per_seq=tps),
        grid=(n_tiles + 1,),
        in_specs=[pl.BlockSpec((tm, D), lambda i: (jnp.minimum(i, n_tiles - 1), 0)),
                  pl.BlockSpec((tm, D), prev_tile)] + [_resident(c.shape) for c in consts],
        out_specs=pl.BlockSpec((tm, D), prev_tile),
        out_shape=jax.ShapeDtypeStruct((N, D), F32),
        scratch_shapes=[
            pltpu.VMEM((CONV_HIST + tm, D), F32),
            pltpu.VMEM((SUBLANES - 1, tm + CONV_HIST - SUBLANES, D), F32),
            pltpu.VMEM((tm, D), F32),
        ],
        compiler_params=_params(1),
        name="conv",
    )(h2, h2, *consts)


def _rope_tables(T, rot):
    half = rot // 2
    inv = ROPE_THETA ** (-jnp.arange(half, dtype=F32) * 2.0 / rot)
    ang = jnp.arange(T).astype(F32)[:, None] * inv[None, :]
    return jnp.cos(ang).T, jnp.sin(ang).T


def kernel(x, mem, norm_mix, norm_cross, norm_mem, norm_ffn, w_in_ab, a_q_norm, a_k_norm, pool_w, pool_scale, w_out_ab, conv_w_in, conv_b_in, conv_dw_w, conv_dw_b, conv_ln_g, conv_ln_b, conv_w_out, cross_wq, cross_wk, cross_wv, cross_q_norm, cross_k_norm, cross_wo, ffn_w_gate, ffn_w_up, ffn_w_down):
    B, T, D = x.shape
    N = B * T
    depth = norm_mix.shape[0]
    aw = A_HEADS * HEAD_DIM
    iqw = IDX_HEADS * IDX_DIM
    pw = len(POOL_WINDOWS) * POOL_GROUP
    assert T % ROW_TILE == 0 and T % KEY_CHUNK == 0 and KEY_CHUNK % QUERY_BLOCK == 0
    assert T % CONV_TILE == 0 and CONV_TILE >= CONV_HIST
    row = lambda v: v.reshape(1, -1)

    kT_mem, v_mem = _mem_kv(mem, norm_mem[:, None, :], cross_wk.astype(BF16),
                            cross_wv.astype(BF16), cross_k_norm[:, None, :])
    cos16, sin16 = _rope_tables(T, HEAD_DIM // 4)
    cos8, sin8 = _rope_tables(T, IDX_DIM // 4)

    h = x.reshape(N, D)
    for l in range(depth):
        if l % 2 == 0:
            e = l // 2
            w_in = w_in_ab[e]
            o_u, o_iq, o_iw, o_ik = 3 * aw, 3 * aw + pw, 3 * aw + pw + iqw, 3 * aw + pw + iqw + IDX_HEADS
            wt = jnp.concatenate(
                [w_in[:, :3 * aw], w_in[:, o_iq:o_iw], w_in[:, o_ik:], w_in[:, o_iw:o_ik]],
                axis=1).T.astype(BF16)
            wu = w_in[:, o_u:o_iq].astype(BF16)
            qT, k, vT, iqT, ik, iwT, b2 = _front(
                h, row(norm_mix[l]), wt, wu, a_q_norm[e][:, None], a_k_norm[e][:, None],
                cos16, sin16, cos8, sin8, pool_w[e].astype(BF16), row(pool_scale[e]), B=B, T=T)
            a = _dsa(qT, k, vT, iqT, ik, iwT, B=B, T=T)
            w_out = w_out_ab[e].astype(BF16)
            stream, mix_weights = (h, a.reshape(N, aw), b2), (w_out[:aw], w_out[aw:])
        else:
            o = l // 2
            h = _conv(h, row(norm_mix[l]), conv_w_in[o].astype(BF16), row(conv_b_in[o]),
                      conv_dw_w[o], row(conv_dw_b[o]), row(conv_ln_g[o]), row(conv_ln_b[o]),
                      conv_w_out[o].astype(BF16), T=T)
            stream, mix_weights = (h,), ()
        h = _cross(stream, mix_weights, row(norm_cross[l]), cross_wq[l].astype(BF16),
                   row(cross_q_norm[l]), kT_mem[l], v_mem[l], cross_wo[l].astype(BF16), T=T)
        h = _ffn(h, row(norm_ffn[l]), ffn_w_gate[l].astype(BF16), ffn_w_up[l].astype(BF16),
                 ffn_w_down[l].astype(BF16))
    return h.reshape(B, T, D)
```

```python
import functools

import jax
import jax.numpy as jnp
from jax import lax
from jax.experimental import pallas as pl
from jax.experimental.pallas import tpu as pltpu

F32 = jnp.float32
BF16 = jnp.bfloat16

EPS = 1e-6
A_HEADS = 4
HEAD_DIM = 128
IDX_HEADS = 16
IDX_DIM = 64
TOPK_MAX = 256
ROPE_THETA = 500000.0
POOL_WINDOWS = (2, 4, 8, 16)
POOL_GROUP = 128
POOL_HIST = 16
C_HEADS = 4
CONV_WIDTH = 31
CONV_HIST = 32
SUBLANES = 8
LANES = 128
BF16_ROWS = 16

VMEM_LIMIT_BYTES = 56 * 1024 * 1024
QUERY_BLOCK = 256
KEY_CHUNK = 512
CONV_TILE = 256
ROW_TILE = 512
FFN_CHUNKS = 2
INT32_MIN = -2 ** 31
MASKED_LOGIT = -1e30
V_PAD = 16
V_ROWS = HEAD_DIM + V_PAD
LOG2_E = 1.4426950408889634
KEY_BITS = 32
LOW_KEY_BITS = 16
LOW_BITS_UNCHECKED = 9
INT32_MAX = 2 ** 31 - 1
HIGH_HALF_MASK = -(1 << LOW_KEY_BITS)
SELECT_ALL_THRESHOLD = -3.0e38


def _params(n_axes):
    return pltpu.CompilerParams(
        dimension_semantics=("arbitrary",) * n_axes,
        vmem_limit_bytes=VMEM_LIMIT_BYTES)


def _resident(shape):
    zeros = (0,) * len(shape)
    return pl.BlockSpec(shape, lambda *_: zeros, pipeline_mode=pl.Buffered(1))


def _rms_rows(x, g):
    ms = jnp.mean(x * x, axis=-1, keepdims=True)
    return x * lax.rsqrt(ms + EPS) * g


def _bdot(a, b):
    return jnp.dot(a, b, preferred_element_type=F32)


def _rope_cols(y, cos, sin, half):
    y1, y2 = y[0:half], y[half:2 * half]
    return jnp.concatenate(
        [y1 * cos - y2 * sin, y2 * cos + y1 * sin, y[2 * half:]], axis=0)


def _front_kernel(x_ref, g_ref, wt_ref, wu_ref, gq_ref, gk_ref, cos16_ref, sin16_ref,
                  cos8_ref, sin8_ref, poolw_ref, pscale_ref,
                  qT_ref, k_ref, vT_ref, iqT_ref, ik_ref, iwT_ref, b_ref, ubuf_ref,
                  *, tiles_per_seq):
    tm = x_ref.shape[0]
    i = pl.program_id(0)
    aw = A_HEADS * HEAD_DIM
    iqw = IDX_HEADS * IDX_DIM

    xn = _rms_rows(x_ref[...], g_ref[...])
    xb = xn.astype(BF16)
    xT = xn.T.astype(BF16)
    cols = _bdot(wt_ref[...], xT)

    cos16, sin16 = cos16_ref[...], sin16_ref[...]
    cos8, sin8 = cos8_ref[...], sin8_ref[...]

    def qk_heads(base, gain):
        heads = []
        for h in range(A_HEADS):
            xh = cols[base + h * HEAD_DIM: base + (h + 1) * HEAD_DIM]
            ms = jnp.sum(xh * xh, axis=0, keepdims=True) * (1.0 / HEAD_DIM)
            y = xh * lax.rsqrt(ms + EPS) * gain
            heads.append(_rope_cols(y, cos16, sin16, HEAD_DIM // 8))
        return jnp.concatenate(heads, axis=0)

    qT_ref[0] = (qk_heads(0, gq_ref[...]) * (HEAD_DIM ** -0.5 * LOG2_E)).astype(BF16)
    k_ref[0] = qk_heads(aw, gk_ref[...]).T.astype(BF16)
    ones_blk = (lax.broadcasted_iota(jnp.int32, (V_PAD, tm), 0) == 0).astype(F32)
    v_rows = []
    for h in range(A_HEADS):
        v_rows += [cols[2 * aw + h * HEAD_DIM:2 * aw + (h + 1) * HEAD_DIM], ones_blk]
    vT_ref[0, 0] = jnp.concatenate(v_rows, axis=0).astype(BF16)

    base = 3 * aw
    iq_heads = [
        _rope_cols(cols[base + h * IDX_DIM: base + (h + 1) * IDX_DIM], cos8, sin8, IDX_DIM // 8)
        for h in range(IDX_HEADS)]
    iqT_ref[0] = jnp.concatenate(iq_heads, axis=0).astype(BF16)

    base = 3 * aw + iqw
    ikT = _rope_cols(cols[base:base + IDX_DIM], cos8, sin8, IDX_DIM // 8)
    ikT = jnp.concatenate([ikT, jnp.zeros_like(ikT)], axis=0)
    ik_ref[0] = ikT.T[:, :IDX_DIM].astype(BF16)

    base = 3 * aw + iqw + IDX_DIM
    iwT_ref[0] = cols[base:base + IDX_HEADS] * (IDX_HEADS ** -0.5 * IDX_DIM ** -0.5)

    u = _bdot(xb, wu_ref[...])
    t_in_seq = (i % tiles_per_seq) * tm

    @pl.when(i % tiles_per_seq == 0)
    def _():
        ubuf_ref[0:POOL_HIST, :] = jnp.zeros((POOL_HIST, u.shape[1]), F32)

    ubuf_ref[POOL_HIST:POOL_HIST + tm, :] = u
    t1 = (lax.broadcasted_iota(jnp.int32, (tm, 1), 0) + t_in_seq + 1).astype(F32)
    outs = []
    for g, w in enumerate(POOL_WINDOWS):
        cs = slice(g * POOL_GROUP, (g + 1) * POOL_GROUP)
        acc = u[:, cs]
        for j in range(1, w):
            acc = acc + ubuf_ref[POOL_HIST - j:POOL_HIST - j + tm, cs]
        pooled = acc / jnp.minimum(t1, float(w)) - u[:, cs]
        outs.append(_bdot(pooled.astype(BF16), poolw_ref[g]))
    b_ref[...] = (jnp.concatenate(outs, axis=-1) * pscale_ref[...]).astype(BF16)
    ubuf_ref[0:POOL_HIST, :] = u[tm - POOL_HIST:tm, :]


def _front(x2, g, wt, wu, gq, gk, cos16, sin16, cos8, sin8, poolw, pscale, *, B, T):
    N, D = x2.shape
    tm = KEY_CHUNK
    tps = T // tm
    aw = A_HEADS * HEAD_DIM
    iqw = IDX_HEADS * IDX_DIM
    pw = len(POOL_WINDOWS) * POOL_GROUP
    seq_map = lambda i: (i // tps, 0, i % tps)
    out_shape = (
        jax.ShapeDtypeStruct((B, aw, T), BF16),
        jax.ShapeDtypeStruct((B, T, aw), BF16),
        jax.ShapeDtypeStruct((B, tps, A_HEADS * V_ROWS, tm), BF16),
        jax.ShapeDtypeStruct((B, iqw, T), BF16),
        jax.ShapeDtypeStruct((B, T, IDX_DIM), BF16),
        jax.ShapeDtypeStruct((B, IDX_HEADS, T), F32),
        jax.ShapeDtypeStruct((N, pw), BF16),
    )
    return pl.pallas_call(
        functools.partial(_front_kernel, tiles_per_seq=tps),
        grid=(N // tm,),
        in_specs=[
            pl.BlockSpec((tm, D), lambda i: (i, 0)),
            _resident(g.shape), _resident(wt.shape), _resident(wu.shape),
            _resident(gq.shape), _resident(gk.shape),
            pl.BlockSpec((HEAD_DIM // 8, tm), lambda i: (0, i % tps)),
            pl.BlockSpec((HEAD_DIM // 8, tm), lambda i: (0, i % tps)),
            pl.BlockSpec((IDX_DIM // 8, tm), lambda i: (0, i % tps)),
            pl.BlockSpec((IDX_DIM // 8, tm), lambda i: (0, i % tps)),
            _resident(poolw.shape), _resident(pscale.shape),
        ],
        out_specs=(
            pl.BlockSpec((1, aw, tm), seq_map),
            pl.BlockSpec((1, tm, aw), lambda i: (i // tps, i % tps, 0)),
            pl.BlockSpec((1, 1, A_HEADS * V_ROWS, tm), lambda i: (i // tps, i % tps, 0, 0)),
            pl.BlockSpec((1, iqw, tm), seq_map),
            pl.BlockSpec((1, tm, IDX_DIM), lambda i: (i // tps, i % tps, 0)),
            pl.BlockSpec((1, IDX_HEADS, tm), seq_map),
            pl.BlockSpec((tm, pw), lambda i: (i, 0)),
        ),
        out_shape=out_shape,
        scratch_shapes=[pltpu.VMEM((POOL_HIST + tm, pw), F32)],
        compiler_params=_params(1),
        name="front",
    )(x2, g, wt, wu, gq, gk, cos16, sin16, cos8, sin8, poolw, pscale)


def _key_to_f32(c):
    bits = c ^ ((c >> (KEY_BITS - 1)) & INT32_MAX)
    return lax.bitcast_convert_type(bits, F32)


def _high_half(x):
    bits = lax.bitcast_convert_type(x, jnp.int32) & jnp.int32(HIGH_HALF_MASK)
    return lax.bitcast_convert_type(bits, F32).astype(BF16)


def _dsa_kernel(iqT_ref, iwT_ref, qT_ref, ik_ref, k_ref, vT_ref, o_ref,
                s_ref, s16_ref, acc_ref, lg_ref, *, topk):
    qb, kc = QUERY_BLOCK, KEY_CHUNK
    row_bits = (s_ref.shape[0] - 1).bit_length()
    qi = pl.program_id(1)
    nblk = ((qi + 1) * qb + kc - 1) // kc

    row_i = lax.broadcasted_iota(jnp.int32, (kc, qb), 0)
    col_i = lax.broadcasted_iota(jnp.int32, (kc, qb), 1)

    def score_body(j, carry):
        s0 = pl.multiple_of(j * kc, kc)
        lhs = ik_ref[0, pl.ds(s0, kc), :]
        for h in range(IDX_HEADS):
            rhs = iqT_ref[0, h * IDX_DIM:(h + 1) * IDX_DIM, :]
            val = jnp.maximum(_bdot(lhs, rhs), 0.0) * iwT_ref[0, h:h + 1, :]
            if h > 0:
                val = s_ref[pl.ds(s0, kc), :] + val
            if h == IDX_HEADS - 1:
                val = jnp.where(row_i + s0 <= col_i + qi * qb, val, -jnp.inf)
                s16_ref[pl.ds(s0, kc), :] = _high_half(val)
            s_ref[pl.ds(s0, kc), :] = val
        return carry

    lax.fori_loop(0, nblk, score_body, 0)

    def count_ge_high(cand16):
        def body(j, acc):
            s0 = pl.multiple_of(j * kc, kc)
            blk = s16_ref[pl.ds(s0, kc), :]
            one = jnp.ones((), BF16)
            parts = [jnp.zeros((BF16_ROWS, qb), BF16) for _ in range(4)]
            for r in range(kc // BF16_ROWS):
                a = parts[r % 4]
                parts[r % 4] = jnp.where(blk[BF16_ROWS * r:BF16_ROWS * (r + 1)] >= cand16, a + one, a)
            part = (parts[0] + parts[1]) + (parts[2] + parts[3])
            return acc + part.astype(F32)

        acc = lax.fori_loop(0, nblk, body, jnp.zeros((BF16_ROWS, qb), F32))
        return jnp.sum(acc, axis=0, keepdims=True)

    def count_ge(cand):
        def body(j, accs):
            s0 = pl.multiple_of(j * kc, kc)
            blk = s_ref[pl.ds(s0, kc), :]
            accs = list(accs)
            for r in range(kc // SUBLANES):
                a = accs[r % len(accs)]
                accs[r % len(accs)] = jnp.where(blk[SUBLANES * r:SUBLANES * (r + 1)] >= cand, a + 1.0, a)
            return tuple(accs)

        accs = lax.fori_loop(0, nblk, body, tuple(jnp.zeros((SUBLANES, qb), F32) for _ in range(4)))
        return jnp.sum(accs[0] + accs[1] + accs[2] + accs[3], axis=0, keepdims=True)

    kf = float(topk)
    def step(c, settled, trial, cnt):
        c = jnp.where(jnp.logical_and(cnt >= kf, settled == 0.0), trial, c)
        return c, jnp.where(cnt == kf, 1.0, settled)

    c, settled = step(jnp.full((1, qb), INT32_MIN, jnp.int32), jnp.zeros((1, qb), F32),
                      jnp.zeros((1, qb), jnp.int32), count_ge_high(jnp.zeros((1, qb), BF16)))

    def high_bit_body(i, state):
        c, settled = state
        trial = c + jnp.left_shift(jnp.int32(1), KEY_BITS - 2 - i)
        return step(c, settled, trial, count_ge_high(_high_half(_key_to_f32(trial))))

    c, settled = lax.fori_loop(0, KEY_BITS - 1 - LOW_KEY_BITS, high_bit_body, (c, settled))

    def low_bit_step(i, c, settled):
        trial = c + jnp.left_shift(jnp.int32(1), LOW_KEY_BITS - 1 - i)
        return step(c, settled, trial, count_ge(_key_to_f32(trial)))

    c, settled = lax.fori_loop(
        0, LOW_BITS_UNCHECKED, lambda i, st: low_bit_step(i, *st), (c, settled))

    def low_bit_cond(state):
        i, _, _, unsettled = state
        return jnp.logical_and(i < LOW_KEY_BITS, unsettled > 0.0)

    def low_bit_body(state):
        i, c, settled, _ = state
        c, settled = low_bit_step(i, c, settled)
        return i + 1, c, settled, jnp.max(1.0 - settled)

    _, c, settled, unsettled = lax.while_loop(
        low_bit_cond, low_bit_body,
        (jnp.int32(LOW_BITS_UNCHECKED), c, settled, jnp.max(1.0 - settled)))
    thr = jnp.where(c == INT32_MIN, SELECT_ALL_THRESHOLD, _key_to_f32(c))

    tied = jnp.logical_and(settled == 0.0, c != INT32_MIN)

    @pl.when(unsettled > 0.0)
    def _():
        def count_rows(pred):
            def body(j, acc):
                s0 = pl.multiple_of(j * kc, kc)
                m = jnp.where(pred(s_ref[pl.ds(s0, kc), :], row_i + s0), 1.0, 0.0)
                return acc + jnp.sum(m, axis=0, keepdims=True)

            return lax.fori_loop(0, nblk, body, jnp.zeros((1, qb), F32))

        need = kf - count_rows(lambda blk, rows: blk > thr)

        def row_bit_body(i, r):
            trial = r + jnp.left_shift(jnp.int32(1), row_bits - 1 - i)
            before = count_rows(lambda blk, rows: jnp.logical_and(blk == thr, rows < trial))
            return jnp.where(before < need, trial, r)

        last = lax.fori_loop(0, row_bits, row_bit_body, jnp.zeros((1, qb), jnp.int32))

        def demote_body(j, carry):
            s0 = pl.multiple_of(j * kc, kc)
            blk = s_ref[pl.ds(s0, kc), :]
            drop = jnp.logical_and(jnp.logical_and(blk == thr, row_i + s0 > last), tied)
            s_ref[pl.ds(s0, kc), :] = jnp.where(drop, -jnp.inf, blk)
            return carry

        lax.fori_loop(0, nblk, demote_body, 0)

    acc_ref[...] = jnp.zeros(acc_ref.shape, F32)

    def logits(jn):
        s0 = pl.multiple_of(jn * kc, kc)
        return [_bdot(k_ref[0, pl.ds(s0, kc), h * HEAD_DIM:(h + 1) * HEAD_DIM],
                      qT_ref[0, h * HEAD_DIM:(h + 1) * HEAD_DIM, :]) for h in range(A_HEADS)]

    def stage_logits(lgs):
        for h in range(A_HEADS):
            lg_ref[h * kc:(h + 1) * kc, :] = lgs[h]

    stage_logits(logits(0))

    def attn_step(j, ms, stage_next):
        if stage_next:
            next_lgs = logits(j + 1)
        s0 = pl.multiple_of(j * kc, kc)
        bias = jnp.where(s_ref[pl.ds(s0, kc), :] >= thr, 0.0, MASKED_LOGIT)
        new_ms = []
        for h in range(A_HEADS):
            vr = slice(h * V_ROWS, (h + 1) * V_ROWS)
            lg = lg_ref[h * kc:(h + 1) * kc, :] + bias
            m_new = jnp.maximum(ms[h], jnp.max(lg, axis=0, keepdims=True))
            p = jnp.exp2(lg - m_new).astype(BF16)
            acc_ref[vr, :] = jnp.exp2(ms[h] - m_new) * acc_ref[vr, :] + _bdot(vT_ref[0, j, vr, :], p)
            new_ms.append(m_new)
        if stage_next:
            stage_logits(next_lgs)
        return tuple(new_ms)

    ms = lax.fori_loop(0, nblk - 1, lambda j, ms: attn_step(j, ms, True),
                       tuple(jnp.full((1, qb), MASKED_LOGIT, F32) for _ in range(A_HEADS)))
    attn_step(nblk - 1, ms, False)
    outs = []
    for h in range(A_HEADS):
        denom = acc_ref[h * V_ROWS + HEAD_DIM:h * V_ROWS + HEAD_DIM + 1, :]
        outs.append(acc_ref[h * V_ROWS:h * V_ROWS + HEAD_DIM, :] * (1.0 / denom))
    o_ref[0] = jnp.concatenate(outs, axis=0).T.astype(BF16)


def _dsa(qT, k, vT, iqT, ik, iwT, *, B, T):
    qb, kc = QUERY_BLOCK, KEY_CHUNK
    aw = A_HEADS * HEAD_DIM
    iqw = IDX_HEADS * IDX_DIM
    topk = min(TOPK_MAX, T // 4)
    blk = lambda b, q: (b, 0, q)
    per_batch = lambda shape: pl.BlockSpec(
        (1,) + shape, lambda b, q: (b,) + (0,) * len(shape), pipeline_mode=pl.Buffered(1))
    return pl.pallas_call(
        functools.partial(_dsa_kernel, topk=topk),
        grid=(B, T // qb),
        in_specs=[
            pl.BlockSpec((1, iqw, qb), blk),
            pl.BlockSpec((1, IDX_HEADS, qb), blk),
            pl.BlockSpec((1, aw, qb), blk),
            per_batch((T, IDX_DIM)),
            per_batch((T, aw)),
            per_batch((T // kc, A_HEADS * V_ROWS, kc)),
        ],
        out_specs=pl.BlockSpec((1, qb, aw), lambda b, q: (b, q, 0)),
        out_shape=jax.ShapeDtypeStruct((B, T, aw), BF16),
        scratch_shapes=[
            pltpu.VMEM((T, qb), F32),
            pltpu.VMEM((T, qb), BF16),
            pltpu.VMEM((A_HEADS * V_ROWS, qb), F32),
            pltpu.VMEM((A_HEADS * kc, qb), F32),
        ],
        compiler_params=_params(2),
        name="dsa",
    )(iqT, iwT, qT, ik, k, vT)


def _mem_kernel(mem_ref, g_ref, wk_ref, wv_ref, gk_ref, kT_ref, v_ref):
    dh = gk_ref.shape[-1]
    mn = _rms_rows(mem_ref[0], g_ref[0]).astype(BF16)
    k = _bdot(mn, wk_ref[0])
    v_ref[0, 0] = _bdot(mn, wv_ref[0]).astype(BF16)
    for c in range(C_HEADS):
        kc = _rms_rows(k[:, c * dh:(c + 1) * dh], gk_ref[0])
        kT_ref[0, 0, c] = kc.T.astype(BF16)


def _mem_kv(mem, g, wk, wv, gk):
    B, M, D = mem.shape
    L = wk.shape[0]
    dh = D // C_HEADS
    lay = lambda *tail: pl.BlockSpec((1,) + tail, lambda l, b: (l,) + (0,) * len(tail))
    return pl.pallas_call(
        _mem_kernel,
        grid=(L, B),
        in_specs=[
            pl.BlockSpec((1, M, D), lambda l, b: (b, 0, 0)),
            lay(1, D), lay(D, D), lay(D, D), lay(1, dh),
        ],
        out_specs=(
            pl.BlockSpec((1, 1, C_HEADS, dh, M), lambda l, b: (l, b, 0, 0, 0)),
            pl.BlockSpec((1, 1, M, D), lambda l, b: (l, b, 0, 0)),
        ),
        out_shape=(
            jax.ShapeDtypeStruct((L, B, C_HEADS, dh, M), BF16),
            jax.ShapeDtypeStruct((L, B, M, D), BF16),
        ),
        compiler_params=_params(2),
        name="mem_kv",
    )(mem, g, wk, wv, gk)


def _cross_kernel(*refs, mix):
    if mix:
        x_ref, a_ref, b_ref, wa_ref, wb_ref = refs[:5]
        h = x_ref[...] + _bdot(a_ref[...], wa_ref[...]) + _bdot(b_ref[...], wb_ref[...])
    else:
        h = refs[0][...]
    g_ref, wq_ref, gq_ref, kT_ref, v_ref, wo_ref, o_ref = refs[-7:]
    dh = gq_ref.shape[-1]
    q = _bdot(_rms_rows(h, g_ref[...]).astype(BF16), wq_ref[...])
    heads = []
    for c in range(C_HEADS):
        cs = slice(c * dh, (c + 1) * dh)
        qc = _rms_rows(q[:, cs], gq_ref[...]).astype(BF16)
        lg = _bdot(qc, kT_ref[0, c]) * dh ** -0.5
        p = jnp.exp(lg - jnp.max(lg, axis=-1, keepdims=True))
        p = p * (1.0 / jnp.sum(p, axis=-1, keepdims=True))
        heads.append(_bdot(p.astype(BF16), v_ref[0, :, cs]))
    o = jnp.concatenate(heads, axis=-1).astype(BF16)
    o_ref[...] = h + _bdot(o, wo_ref[...])


def _cross(stream, mix_weights, g, wq, gq, kT, v, wo, *, T):
    N, D = stream[0].shape
    tm = ROW_TILE
    tps = T // tm
    dh = D // C_HEADS
    M = v.shape[1]
    return pl.pallas_call(
        functools.partial(_cross_kernel, mix=bool(mix_weights)),
        grid=(N // tm,),
        in_specs=[pl.BlockSpec((tm, s.shape[1]), lambda i: (i, 0)) for s in stream]
        + [_resident(w.shape) for w in mix_weights] + [
            _resident(g.shape), _resident(wq.shape), _resident(gq.shape),
            pl.BlockSpec((1, C_HEADS, dh, M), lambda i: (i // tps, 0, 0, 0)),
            pl.BlockSpec((1, M, D), lambda i: (i // tps, 0, 0)),
            _resident(wo.shape),
        ],
        out_specs=pl.BlockSpec((tm, D), lambda i: (i, 0)),
        out_shape=jax.ShapeDtypeStruct((N, D), F32),
        compiler_params=_params(1),
        name="cross",
    )(*stream, *mix_weights, g, wq, gq, kT, v, wo)


def _ffn_kernel(h_ref, g_ref, wg_ref, wu_ref, wd_ref, o_ref, *, n_chunks):
    h = h_ref[...]
    hn = _rms_rows(h, g_ref[...]).astype(BF16)
    fc = wg_ref.shape[1] // n_chunks
    acc = h
    for c in range(n_chunks):
        cs = slice(c * fc, (c + 1) * fc)
        gate = _bdot(hn, wg_ref[:, cs])
        act = gate * jax.nn.sigmoid(gate) * _bdot(hn, wu_ref[:, cs])
        acc = acc + _bdot(act.astype(BF16), wd_ref[cs, :])
    o_ref[...] = acc


def _ffn(h2, g, wg, wu, wd):
    N, D = h2.shape
    tm = ROW_TILE
    return pl.pallas_call(
        functools.partial(_ffn_kernel, n_chunks=FFN_CHUNKS),
        grid=(N // tm,),
        in_specs=[
            pl.BlockSpec((tm, D), lambda i: (i, 0)),
            _resident(g.shape), _resident(wg.shape), _resident(wu.shape), _resident(wd.shape),
        ],
        out_specs=pl.BlockSpec((tm, D), lambda i: (i, 0)),
        out_shape=jax.ShapeDtypeStruct((N, D), F32),
        compiler_params=_params(1),
        name="ffn",
    )(h2, g, wg, wu, wd)


def _conv_kernel(h_ref, hprev_ref, g_ref, win_ref, bin_ref, dw_ref, dwb_ref, lng_ref, lnb_ref,
                 wout_ref, o_ref, cbuf_ref, shift_ref, gnext_ref, *, tiles_per_seq):
    tm, D = h_ref.shape
    i = pl.program_id(0)

    @pl.when(i == 0)
    def _():
        cbuf_ref[...] = jnp.zeros(cbuf_ref.shape, F32)

    @pl.when((i + tiles_per_seq - 1) % tiles_per_seq == 0)
    def _():
        cbuf_ref[0:CONV_HIST, :] = jnp.zeros((CONV_HIST, D), F32)

    hh = _bdot(_rms_rows(h_ref[...], g_ref[...]).astype(BF16), win_ref[...]) + bin_ref[...]
    gnext_ref[...] = hh[:, :D] * jax.nn.sigmoid(hh[:, D:])

    off = CONV_HIST - (CONV_WIDTH - 1)
    span = tm + CONV_HIST - SUBLANES
    for r in range(1, SUBLANES):
        shift_ref[r - 1] = cbuf_ref[r:r + span, :]
    blocks = []
    for cb in range(D // LANES):
        cs = slice(cb * LANES, (cb + 1) * LANES)
        acc = None
        for j in range(CONV_WIDTH):
            a, r = divmod(off + j, SUBLANES)
            src = cbuf_ref if r == 0 else shift_ref.at[r - 1]
            term = src[a * SUBLANES:a * SUBLANES + tm, cs] * dw_ref[j:j + 1, cs]
            acc = term if acc is None else acc + term
        blocks.append(acc)

    y = jnp.concatenate(blocks, axis=-1) + dwb_ref[...]
    mu = jnp.mean(y, axis=-1, keepdims=True)
    yc = y - mu
    var = jnp.mean(yc * yc, axis=-1, keepdims=True)
    yn = yc * lax.rsqrt(var + EPS) * lng_ref[...] + lnb_ref[...]
    act = yn * jax.nn.sigmoid(yn)
    o_ref[...] = hprev_ref[...] + _bdot(act.astype(BF16), wout_ref[...])

    cbuf_ref[0:CONV_HIST, :] = cbuf_ref[tm:tm + CONV_HIST, :]
    cbuf_ref[CONV_HIST:CONV_HIST + tm, :] = gnext_ref[...]


def _conv(h2, g, win, bin_, dw, dwb, lng, lnb, wout, *, T):
    N, D = h2.shape
    tm = CONV_TILE
    tps = T // tm
    consts = (g, win, bin_, dw, dwb, lng, lnb, wout)
    n_tiles = N // tm
    prev_tile = lambda i: (jnp.maximum(i - 1, 0), 0)
    return pl.pallas_call(
        functools.partial(_conv_kernel, tiles_per_seq=tps),
        grid=(n_tiles + 1,),
        in_specs=[pl.BlockSpec((tm, D), lambda i: (jnp.minimum(i, n_tiles - 1), 0)),
                  pl.BlockSpec((tm, D), prev_tile)] + [_resident(c.shape) for c in consts],
        out_specs=pl.BlockSpec((tm, D), prev_tile),
        out_shape=jax.ShapeDtypeStruct((N, D), F32),
        scratch_shapes=[
            pltpu.VMEM((CONV_HIST + tm, D), F32),
            pltpu.VMEM((SUBLANES - 1, tm + CONV_HIST - SUBLANES, D), F32),
            pltpu.VMEM((tm, D), F32),
        ],
        compiler_params=_params(1),
        name="conv",
    )(h2, h2, *consts)


def _rope_tables(T, rot):
    half = rot // 2
    inv = ROPE_THETA ** (-jnp.arange(half, dtype=F32) * 2.0 / rot)
    ang = jnp.arange(T).astype(F32)[:, None] * inv[None, :]
    return jnp.cos(ang).T, jnp.sin(ang).T


def kernel(x, mem, norm_mix, norm_cross, norm_mem, norm_ffn, w_in_ab, a_q_norm, a_k_norm, pool_w, pool_scale, w_out_ab, conv_w_in, conv_b_in, conv_dw_w, conv_dw_b, conv_ln_g, conv_ln_b, conv_w_out, cross_wq, cross_wk, cross_wv, cross_q_norm, cross_k_norm, cross_wo, ffn_w_gate, ffn_w_up, ffn_w_down):
    B, T, D = x.shape
    N = B * T
    depth = norm_mix.shape[0]
    aw = A_HEADS * HEAD_DIM
    iqw = IDX_HEADS * IDX_DIM
    pw = len(POOL_WINDOWS) * POOL_GROUP
    assert T % ROW_TILE == 0 and T % KEY_CHUNK == 0 and KEY_CHUNK % QUERY_BLOCK == 0
    assert T % CONV_TILE == 0 and CONV_TILE >= CONV_HIST
    row = lambda v: v.reshape(1, -1)

    kT_mem, v_mem = _mem_kv(mem, norm_mem[:, None, :], cross_wk.astype(BF16),
                            cross_wv.astype(BF16), cross_k_norm[:, None, :])
    cos16, sin16 = _rope_tables(T, HEAD_DIM // 4)
    cos8, sin8 = _rope_tables(T, IDX_DIM // 4)

    h = x.reshape(N, D)
    for l in range(depth):
        if l % 2 == 0:
            e = l // 2
            w_in = w_in_ab[e]
            o_u, o_iq, o_iw, o_ik = 3 * aw, 3 * aw + pw, 3 * aw + pw + iqw, 3 * aw + pw + iqw + IDX_HEADS
            wt = jnp.concatenate(
                [w_in[:, :3 * aw], w_in[:, o_iq:o_iw], w_in[:, o_ik:], w_in[:, o_iw:o_ik]],
                axis=1).T.astype(BF16)
            wu = w_in[:, o_u:o_iq].astype(BF16)
            qT, k, vT, iqT, ik, iwT, b2 = _front(
                h, row(norm_mix[l]), wt, wu, a_q_norm[e][:, None], a_k_norm[e][:, None],
                cos16, sin16, cos8, sin8, pool_w[e].astype(BF16), row(pool_scale[e]), B=B, T=T)
            a = _dsa(qT, k, vT, iqT, ik, iwT, B=B, T=T)
            w_out = w_out_ab[e].astype(BF16)
            stream, mix_weights = (h, a.reshape(N, aw), b2), (w_out[:aw], w_out[aw:])
        else:
            o = l // 2
            h = _conv(h, row(norm_mix[l]), conv_w_in[o].astype(BF16), row(conv_b_in[o]),
                      conv_dw_w[o], row(conv_dw_b[o]), row(conv_ln_g[o]), row(conv_ln_b[o]),
                      conv_w_out[o].astype(BF16), T=T)
            stream, mix_weights = (h,), ()
        h = _cross(stream, mix_weights, row(norm_cross[l]), cross_wq[l].astype(BF16),
                   row(cross_q_norm[l]), kT_mem[l], v_mem[l], cross_wo[l].astype(BF16), T=T)
        h = _ffn(h, row(norm_ffn[l]), ffn_w_gate[l].astype(BF16), ffn_w_up[l].astype(BF16),
                 ffn_w_down[l].astype(BF16))
    return h.reshape(B, T, D)
```

```python
import functools

import jax
import jax.numpy as jnp
from jax import lax
from jax.experimental import pallas as pl
from jax.experimental.pallas import tpu as pltpu

F32 = jnp.float32
BF16 = jnp.bfloat16

EPS = 1e-6
A_HEADS = 4
HEAD_DIM = 128
IDX_HEADS = 16
IDX_DIM = 64
TOPK_MAX = 256
ROPE_THETA = 500000.0
POOL_WINDOWS = (2, 4, 8, 16)
POOL_GROUP = 128
POOL_HIST = 16
C_HEADS = 4
CONV_WIDTH = 31
CONV_HIST = 32
SUBLANES = 8
LANES = 128
BF16_ROWS = 16

VMEM_LIMIT_BYTES = 56 * 1024 * 1024
QUERY_BLOCK = 256
KEY_CHUNK = 512
CONV_TILE = 256
ROW_TILE = 512
FFN_CHUNKS = 2
INT32_MIN = -2 ** 31
MASKED_LOGIT = -1e30
V_PAD = 16
V_ROWS = HEAD_DIM + V_PAD
LOG2_E = 1.4426950408889634
KEY_BITS = 32
LOW_KEY_BITS = 16
LOW_BITS_UNCHECKED = 9
INT32_MAX = 2 ** 31 - 1
HIGH_HALF_MASK = -(1 << LOW_KEY_BITS)
SELECT_ALL_THRESHOLD = -3.0e38


def _params(n_axes):
    return pltpu.CompilerParams(
        dimension_semantics=("arbitrary",) * n_axes,
        vmem_limit_bytes=VMEM_LIMIT_BYTES)


def _resident(shape):
    zeros = (0,) * len(shape)
    return pl.BlockSpec(shape, lambda *_: zeros, pipeline_mode=pl.Buffered(1))


def _rms_rows(x, g):
    ms = jnp.mean(x * x, axis=-1, keepdims=True)
    return x * lax.rsqrt(ms + EPS) * g


def _bdot(a, b):
    return jnp.dot(a, b, preferred_element_type=F32)


def _rope_cols(y, cos, sin, half):
    y1, y2 = y[0:half], y[half:2 * half]
    return jnp.concatenate(
        [y1 * cos - y2 * sin, y2 * cos + y1 * sin, y[2 * half:]], axis=0)


def _front_kernel(x_ref, g_ref, wt_ref, wu_ref, gq_ref, gk_ref, cos16_ref, sin16_ref,
                  cos8_ref, sin8_ref, poolw_ref, pscale_ref,
                  qT_ref, k_ref, vT_ref, iqT_ref, ik_ref, iwT_ref, b_ref, ubuf_ref,
                  *, tiles_per_seq):
    tm = x_ref.shape[0]
    i = pl.program_id(0)
    aw = A_HEADS * HEAD_DIM
    iqw = IDX_HEADS * IDX_DIM

    xn = _rms_rows(x_ref[...], g_ref[...])
    xb = xn.astype(BF16)
    xT = xn.T.astype(BF16)
    cols = _bdot(wt_ref[...], xT)

    cos16, sin16 = cos16_ref[...], sin16_ref[...]
    cos8, sin8 = cos8_ref[...], sin8_ref[...]

    def qk_heads(base, gain):
        heads = []
        for h in range(A_HEADS):
            xh = cols[base + h * HEAD_DIM: base + (h + 1) * HEAD_DIM]
            ms = jnp.sum(xh * xh, axis=0, keepdims=True) * (1.0 / HEAD_DIM)
            y = xh * lax.rsqrt(ms + EPS) * gain
            heads.append(_rope_cols(y, cos16, sin16, HEAD_DIM // 8))
        return jnp.concatenate(heads, axis=0)

    qT_ref[0] = (qk_heads(0, gq_ref[...]) * (HEAD_DIM ** -0.5 * LOG2_E)).astype(BF16)
    k_ref[0] = qk_heads(aw, gk_ref[...]).T.astype(BF16)
    ones_blk = (lax.broadcasted_iota(jnp.int32, (V_PAD, tm), 0) == 0).astype(F32)
    v_rows = []
    for h in range(A_HEADS):
        v_rows += [cols[2 * aw + h * HEAD_DIM:2 * aw + (h + 1) * HEAD_DIM], ones_blk]
    vT_ref[0, 0] = jnp.concatenate(v_rows, axis=0).astype(BF16)

    base = 3 * aw
    iq_heads = [
        _rope_cols(cols[base + h * IDX_DIM: base + (h + 1) * IDX_DIM], cos8, sin8, IDX_DIM // 8)
        for h in range(IDX_HEADS)]
    iqT_ref[0] = jnp.concatenate(iq_heads, axis=0).astype(BF16)

    base = 3 * aw + iqw
    ikT = _rope_cols(cols[base:base + IDX_DIM], cos8, sin8, IDX_DIM // 8)
    ikT = jnp.concatenate([ikT, jnp.zeros_like(ikT)], axis=0)
    ik_ref[0] = ikT.T[:, :IDX_DIM].astype(BF16)

    base = 3 * aw + iqw + IDX_DIM
    iwT_ref[0] = cols[base:base + IDX_HEADS] * (IDX_HEADS ** -0.5 * IDX_DIM ** -0.5)

    u = _bdot(xb, wu_ref[...])
    t_in_seq = (i % tiles_per_seq) * tm

    @pl.when(i % tiles_per_seq == 0)
    def _():
        ubuf_ref[0:POOL_HIST, :] = jnp.zeros((POOL_HIST, u.shape[1]), F32)

    ubuf_ref[POOL_HIST:POOL_HIST + tm, :] = u
    t1 = (lax.broadcasted_iota(jnp.int32, (tm, 1), 0) + t_in_seq + 1).astype(F32)
    outs = []
    for g, w in enumerate(POOL_WINDOWS):
        cs = slice(g * POOL_GROUP, (g + 1) * POOL_GROUP)
        acc = u[:, cs]
        for j in range(1, w):
            acc = acc + ubuf_ref[POOL_HIST - j:POOL_HIST - j + tm, cs]
        pooled = acc / jnp.minimum(t1, float(w)) - u[:, cs]
        outs.append(_bdot(pooled.astype(BF16), poolw_ref[g]))
    b_ref[...] = (jnp.concatenate(outs, axis=-1) * pscale_ref[...]).astype(BF16)
    ubuf_ref[0:POOL_HIST, :] = u[tm - POOL_HIST:tm, :]


def _front(x2, g, wt, wu, gq, gk, cos16, sin16, cos8, sin8, poolw, pscale, *, B, T):
    N, D = x2.shape
    tm = KEY_CHUNK
    tps = T // tm
    aw = A_HEADS * HEAD_DIM
    iqw = IDX_HEADS * IDX_DIM
    pw = len(POOL_WINDOWS) * POOL_GROUP
    seq_map = lambda i: (i // tps, 0, i % tps)
    out_shape = (
        jax.ShapeDtypeStruct((B, aw, T), BF16),
        jax.ShapeDtypeStruct((B, T, aw), BF16),
        jax.ShapeDtypeStruct((B, tps, A_HEADS * V_ROWS, tm), BF16),
        jax.ShapeDtypeStruct((B, iqw, T), BF16),
        jax.ShapeDtypeStruct((B, T, IDX_DIM), BF16),
        jax.ShapeDtypeStruct((B, IDX_HEADS, T), F32),
        jax.ShapeDtypeStruct((N, pw), BF16),
    )
    return pl.pallas_call(
        functools.partial(_front_kernel, tiles_per_seq=tps),
        grid=(N // tm,),
        in_specs=[
            pl.BlockSpec((tm, D), lambda i: (i, 0)),
            _resident(g.shape), _resident(wt.shape), _resident(wu.shape),
            _resident(gq.shape), _resident(gk.shape),
            pl.BlockSpec((HEAD_DIM // 8, tm), lambda i: (0, i % tps)),
            pl.BlockSpec((HEAD_DIM // 8, tm), lambda i: (0, i % tps)),
            pl.BlockSpec((IDX_DIM // 8, tm), lambda i: (0, i % tps)),
            pl.BlockSpec((IDX_DIM // 8, tm), lambda i: (0, i % tps)),
            _resident(poolw.shape), _resident(pscale.shape),
        ],
        out_specs=(
            pl.BlockSpec((1, aw, tm), seq_map),
            pl.BlockSpec((1, tm, aw), lambda i: (i // tps, i % tps, 0)),
            pl.BlockSpec((1, 1, A_HEADS * V_ROWS, tm), lambda i: (i // tps, i % tps, 0, 0)),
            pl.BlockSpec((1, iqw, tm), seq_map),
            pl.BlockSpec((1, tm, IDX_DIM), lambda i: (i // tps, i % tps, 0)),
            pl.BlockSpec((1, IDX_HEADS, tm), seq_map),
            pl.BlockSpec((tm, pw), lambda i: (i, 0)),
        ),
        out_shape=out_shape,
        scratch_shapes=[pltpu.VMEM((POOL_HIST + tm, pw), F32)],
        compiler_params=_params(1),
        name="front",
    )(x2, g, wt, wu, gq, gk, cos16, sin16, cos8, sin8, poolw, pscale)


def _key_to_f32(c):
    bits = c ^ ((c >> (KEY_BITS - 1)) & INT32_MAX)
    return lax.bitcast_convert_type(bits, F32)


def _high_half(x):
    bits = lax.bitcast_convert_type(x, jnp.int32) & jnp.int32(HIGH_HALF_MASK)
    return lax.bitcast_convert_type(bits, F32).astype(BF16)


def _dsa_kernel(iqT_ref, iwT_ref, qT_ref, ik_ref, k_ref, vT_ref, o_ref,
                s_ref, s16_ref, acc_ref, lg_ref, *, topk):
    qb, kc = QUERY_BLOCK, KEY_CHUNK
    row_bits = (s_ref.shape[0] - 1).bit_length()
    qi = pl.program_id(1)
    nblk = ((qi + 1) * qb + kc - 1) // kc

    row_i = lax.broadcasted_iota(jnp.int32, (kc, qb), 0)
    col_i = lax.broadcasted_iota(jnp.int32, (kc, qb), 1)

    def score_body(j, carry):
        s0 = pl.multiple_of(j * kc, kc)
        lhs = ik_ref[0, pl.ds(s0, kc), :]
        for h in range(IDX_HEADS):
            rhs = iqT_ref[0, h * IDX_DIM:(h + 1) * IDX_DIM, :]
            val = jnp.maximum(_bdot(lhs, rhs), 0.0) * iwT_ref[0, h:h + 1, :]
            if h == 0:
                val = jnp.where(row_i + s0 <= col_i + qi * qb, val, -jnp.inf)
            else:
                val = s_ref[pl.ds(s0, kc), :] + val
            if h == IDX_HEADS - 1:
                s16_ref[pl.ds(s0, kc), :] = _high_half(val)
            s_ref[pl.ds(s0, kc), :] = val
        return carry

    lax.fori_loop(0, nblk, score_body, 0)

    def count_ge_high(cand16):
        def body(j, acc):
            s0 = pl.multiple_of(j * kc, kc)
            blk = s16_ref[pl.ds(s0, kc), :]
            one = jnp.ones((), BF16)
            parts = [jnp.zeros((BF16_ROWS, qb), BF16) for _ in range(4)]
            for r in range(kc // BF16_ROWS):
                a = parts[r % 4]
                parts[r % 4] = jnp.where(blk[BF16_ROWS * r:BF16_ROWS * (r + 1)] >= cand16, a + one, a)
            part = (parts[0] + parts[1]) + (parts[2] + parts[3])
            return acc + part.astype(F32)

        acc = lax.fori_loop(0, nblk, body, jnp.zeros((BF16_ROWS, qb), F32))
        return jnp.sum(acc, axis=0, keepdims=True)

    def count_ge(cand):
        def body(j, accs):
            s0 = pl.multiple_of(j * kc, kc)
            blk = s_ref[pl.ds(s0, kc), :]
            accs = list(accs)
            for r in range(kc // SUBLANES):
                a = accs[r % len(accs)]
                accs[r % len(accs)] = jnp.where(blk[SUBLANES * r:SUBLANES * (r + 1)] >= cand, a + 1.0, a)
            return tuple(accs)

        accs = lax.fori_loop(0, nblk, body, tuple(jnp.zeros((SUBLANES, qb), F32) for _ in range(4)))
        return jnp.sum(accs[0] + accs[1] + accs[2] + accs[3], axis=0, keepdims=True)

    kf = float(topk)
    def step(c, settled, trial, cnt):
        c = jnp.where(jnp.logical_and(cnt >= kf, settled == 0.0), trial, c)
        return c, jnp.where(cnt == kf, 1.0, settled)

    c, settled = step(jnp.full((1, qb), INT32_MIN, jnp.int32), jnp.zeros((1, qb), F32),
                      jnp.zeros((1, qb), jnp.int32), count_ge_high(jnp.zeros((1, qb), BF16)))

    def high_bit_body(i, state):
        c, settled = state
        trial = c + jnp.left_shift(jnp.int32(1), KEY_BITS - 2 - i)
        return step(c, settled, trial, count_ge_high(_high_half(_key_to_f32(trial))))

    c, settled = lax.fori_loop(0, KEY_BITS - 1 - LOW_KEY_BITS, high_bit_body, (c, settled))

    def low_bit_step(i, c, settled):
        trial = c + jnp.left_shift(jnp.int32(1), LOW_KEY_BITS - 1 - i)
        return step(c, settled, trial, count_ge(_key_to_f32(trial)))

    c, settled = lax.fori_loop(
        0, LOW_BITS_UNCHECKED, lambda i, st: low_bit_step(i, *st), (c, settled))

    def low_bit_cond(state):
        i, _, _, unsettled = state
        return jnp.logical_and(i < LOW_KEY_BITS, unsettled > 0.0)

    def low_bit_body(state):
        i, c, settled, _ = state
        c, settled = low_bit_step(i, c, settled)
        return i + 1, c, settled, jnp.max(1.0 - settled)

    _, c, settled, unsettled = lax.while_loop(
        low_bit_cond, low_bit_body,
        (jnp.int32(LOW_BITS_UNCHECKED), c, settled, jnp.max(1.0 - settled)))
    thr = jnp.where(c == INT32_MIN, SELECT_ALL_THRESHOLD, _key_to_f32(c))

    tied = jnp.logical_and(settled == 0.0, c != INT32_MIN)

    @pl.when(unsettled > 0.0)
    def _():
        def count_rows(pred):
            def body(j, acc):
                s0 = pl.multiple_of(j * kc, kc)
                m = jnp.where(pred(s_ref[pl.ds(s0, kc), :], row_i + s0), 1.0, 0.0)
                return acc + jnp.sum(m, axis=0, keepdims=True)

            return lax.fori_loop(0, nblk, body, jnp.zeros((1, qb), F32))

        need = kf - count_rows(lambda blk, rows: blk > thr)

        def row_bit_body(i, r):
            trial = r + jnp.left_shift(jnp.int32(1), row_bits - 1 - i)
            before = count_rows(lambda blk, rows: jnp.logical_and(blk == thr, rows < trial))
            return jnp.where(before < need, trial, r)

        last = lax.fori_loop(0, row_bits, row_bit_body, jnp.zeros((1, qb), jnp.int32))

        def demote_body(j, carry):
            s0 = pl.multiple_of(j * kc, kc)
            blk = s_ref[pl.ds(s0, kc), :]
            drop = jnp.logical_and(jnp.logical_and(blk == thr, row_i + s0 > last), tied)
            s_ref[pl.ds(s0, kc), :] = jnp.where(drop, -jnp.inf, blk)
            return carry

        lax.fori_loop(0, nblk, demote_body, 0)

    acc_ref[...] = jnp.zeros(acc_ref.shape, F32)

    def logits(jn):
        s0 = pl.multiple_of(jn * kc, kc)
        return [_bdot(k_ref[0, pl.ds(s0, kc), h * HEAD_DIM:(h + 1) * HEAD_DIM],
                      qT_ref[0, h * HEAD_DIM:(h + 1) * HEAD_DIM, :]) for h in range(A_HEADS)]

    def stage_logits(lgs):
        for h in range(A_HEADS):
            lg_ref[h * kc:(h + 1) * kc, :] = lgs[h]

    stage_logits(logits(0))

    def attn_step(j, ms, stage_next):
        if stage_next:
            next_lgs = logits(j + 1)
        s0 = pl.multiple_of(j * kc, kc)
        bias = jnp.where(s_ref[pl.ds(s0, kc), :] >= thr, 0.0, MASKED_LOGIT)
        new_ms = []
        for h in range(A_HEADS):
            vr = slice(h * V_ROWS, (h + 1) * V_ROWS)
            lg = lg_ref[h * kc:(h + 1) * kc, :] + bias
            m_new = jnp.maximum(ms[h], jnp.max(lg, axis=0, keepdims=True))
            p = jnp.exp2(lg - m_new).astype(BF16)
            acc_ref[vr, :] = jnp.exp2(ms[h] - m_new) * acc_ref[vr, :] + _bdot(vT_ref[0, j, vr, :], p)
            new_ms.append(m_new)
        if stage_next:
            stage_logits(next_lgs)
        return tuple(new_ms)

    ms = lax.fori_loop(0, nblk - 1, lambda j, ms: attn_step(j, ms, True),
                       tuple(jnp.full((1, qb), MASKED_LOGIT, F32) for _ in range(A_HEADS)))
    attn_step(nblk - 1, ms, False)
    outs = []
    for h in range(A_HEADS):
        denom = acc_ref[h * V_ROWS + HEAD_DIM:h * V_ROWS + HEAD_DIM + 1, :]
        outs.append(acc_ref[h * V_ROWS:h * V_ROWS + HEAD_DIM, :] * (1.0 / denom))
    o_ref[0] = jnp.concatenate(outs, axis=0).T.astype(BF16)


def _dsa(qT, k, vT, iqT, ik, iwT, *, B, T):
    qb, kc = QUERY_BLOCK, KEY_CHUNK
    aw = A_HEADS * HEAD_DIM
    iqw = IDX_HEADS * IDX_DIM
    topk = min(TOPK_MAX, T // 4)
    blk = lambda b, q: (b, 0, q)
    per_batch = lambda shape: pl.BlockSpec(
        (1,) + shape, lambda b, q: (b,) + (0,) * len(shape), pipeline_mode=pl.Buffered(1))
    return pl.pallas_call(
        functools.partial(_dsa_kernel, topk=topk),
        grid=(B, T // qb),
        in_specs=[
            pl.BlockSpec((1, iqw, qb), blk),
            pl.BlockSpec((1, IDX_HEADS, qb), blk),
            pl.BlockSpec((1, aw, qb), blk),
            per_batch((T, IDX_DIM)),
            per_batch((T, aw)),
            per_batch((T // kc, A_HEADS * V_ROWS, kc)),
        ],
        out_specs=pl.BlockSpec((1, qb, aw), lambda b, q: (b, q, 0)),
        out_shape=jax.ShapeDtypeStruct((B, T, aw), BF16),
        scratch_shapes=[
            pltpu.VMEM((T, qb), F32),
            pltpu.VMEM((T, qb), BF16),
            pltpu.VMEM((A_HEADS * V_ROWS, qb), F32),
            pltpu.VMEM((A_HEADS * kc, qb), F32),
        ],
        compiler_params=_params(2),
        name="dsa",
    )(iqT, iwT, qT, ik, k, vT)


def _mem_kernel(mem_ref, g_ref, wk_ref, wv_ref, gk_ref, kT_ref, v_ref):
    dh = gk_ref.shape[-1]
    mn = _rms_rows(mem_ref[0], g_ref[0]).astype(BF16)
    k = _bdot(mn, wk_ref[0])
    v_ref[0, 0] = _bdot(mn, wv_ref[0]).astype(BF16)
    for c in range(C_HEADS):
        kc = _rms_rows(k[:, c * dh:(c + 1) * dh], gk_ref[0])
        kT_ref[0, 0, c] = kc.T.astype(BF16)


def _mem_kv(mem, g, wk, wv, gk):
    B, M, D = mem.shape
    L = wk.shape[0]
    dh = D // C_HEADS
    lay = lambda *tail: pl.BlockSpec((1,) + tail, lambda l, b: (l,) + (0,) * len(tail))
    return pl.pallas_call(
        _mem_kernel,
        grid=(L, B),
        in_specs=[
            pl.BlockSpec((1, M, D), lambda l, b: (b, 0, 0)),
            lay(1, D), lay(D, D), lay(D, D), lay(1, dh),
        ],
        out_specs=(
            pl.BlockSpec((1, 1, C_HEADS, dh, M), lambda l, b: (l, b, 0, 0, 0)),
            pl.BlockSpec((1, 1, M, D), lambda l, b: (l, b, 0, 0)),
        ),
        out_shape=(
            jax.ShapeDtypeStruct((L, B, C_HEADS, dh, M), BF16),
            jax.ShapeDtypeStruct((L, B, M, D), BF16),
        ),
        compiler_params=_params(2),
        name="mem_kv",
    )(mem, g, wk, wv, gk)


def _cross_kernel(*refs, mix):
    if mix:
        x_ref, a_ref, b_ref, wa_ref, wb_ref = refs[:5]
        h = x_ref[...] + _bdot(a_ref[...], wa_ref[...]) + _bdot(b_ref[...], wb_ref[...])
    else:
        h = refs[0][...]
    g_ref, wq_ref, gq_ref, kT_ref, v_ref, wo_ref, o_ref = refs[-7:]
    dh = gq_ref.shape[-1]
    q = _bdot(_rms_rows(h, g_ref[...]).astype(BF16), wq_ref[...])
    heads = []
    for c in range(C_HEADS):
        cs = slice(c * dh, (c + 1) * dh)
        qc = _rms_rows(q[:, cs], gq_ref[...]).astype(BF16)
        lg = _bdot(qc, kT_ref[0, c]) * dh ** -0.5
        p = jnp.exp(lg - jnp.max(lg, axis=-1, keepdims=True))
        p = p * (1.0 / jnp.sum(p, axis=-1, keepdims=True))
        heads.append(_bdot(p.astype(BF16), v_ref[0, :, cs]))
    o = jnp.concatenate(heads, axis=-1).astype(BF16)
    o_ref[...] = h + _bdot(o, wo_ref[...])


def _cross(stream, mix_weights, g, wq, gq, kT, v, wo, *, T):
    N, D = stream[0].shape
    tm = ROW_TILE
    tps = T // tm
    dh = D // C_HEADS
    M = v.shape[1]
    return pl.pallas_call(
        functools.partial(_cross_kernel, mix=bool(mix_weights)),
        grid=(N // tm,),
        in_specs=[pl.BlockSpec((tm, s.shape[1]), lambda i: (i, 0)) for s in stream]
        + [_resident(w.shape) for w in mix_weights] + [
            _resident(g.shape), _resident(wq.shape), _resident(gq.shape),
            pl.BlockSpec((1, C_HEADS, dh, M), lambda i: (i // tps, 0, 0, 0)),
            pl.BlockSpec((1, M, D), lambda i: (i // tps, 0, 0)),
            _resident(wo.shape),
        ],
        out_specs=pl.BlockSpec((tm, D), lambda i: (i, 0)),
        out_shape=jax.ShapeDtypeStruct((N, D), F32),
        compiler_params=_params(1),
        name="cross",
    )(*stream, *mix_weights, g, wq, gq, kT, v, wo)


def _ffn_kernel(h_ref, g_ref, wg_ref, wu_ref, wd_ref, o_ref, *, n_chunks):
    h = h_ref[...]
    hn = _rms_rows(h, g_ref[...]).astype(BF16)
    fc = wg_ref.shape[1] // n_chunks
    acc = h
    for c in range(n_chunks):
        cs = slice(c * fc, (c + 1) * fc)
        gate = _bdot(hn, wg_ref[:, cs])
        act = gate * jax.nn.sigmoid(gate) * _bdot(hn, wu_ref[:, cs])
        acc = acc + _bdot(act.astype(BF16), wd_ref[cs, :])
    o_ref[...] = acc


def _ffn(h2, g, wg, wu, wd):
    N, D = h2.shape
    tm = ROW_TILE
    return pl.pallas_call(
        functools.partial(_ffn_kernel, n_chunks=FFN_CHUNKS),
        grid=(N // tm,),
        in_specs=[
            pl.BlockSpec((tm, D), lambda i: (i, 0)),
            _resident(g.shape), _resident(wg.shape), _resident(wu.shape), _resident(wd.shape),
        ],
        out_specs=pl.BlockSpec((tm, D), lambda i: (i, 0)),
        out_shape=jax.ShapeDtypeStruct((N, D), F32),
        compiler_params=_params(1),
        name="ffn",
    )(h2, g, wg, wu, wd)


def _conv_kernel(h_ref, hprev_ref, g_ref, win_ref, bin_ref, dw_ref, dwb_ref, lng_ref, lnb_ref,
                 wout_ref, o_ref, cbuf_ref, shift_ref, gnext_ref, *, tiles_per_seq):
    tm, D = h_ref.shape
    i = pl.program_id(0)

    @pl.when(i == 0)
    def _():
        cbuf_ref[...] = jnp.zeros(cbuf_ref.shape, F32)

    @pl.when((i + tiles_per_seq - 1) % tiles_per_seq == 0)
    def _():
        cbuf_ref[0:CONV_HIST, :] = jnp.zeros((CONV_HIST, D), F32)

    hh = _bdot(_rms_rows(h_ref[...], g_ref[...]).astype(BF16), win_ref[...]) + bin_ref[...]
    gnext_ref[...] = hh[:, :D] * jax.nn.sigmoid(hh[:, D:])

    off = CONV_HIST - (CONV_WIDTH - 1)
    span = tm + CONV_HIST - SUBLANES
    for r in range(1, SUBLANES):
        shift_ref[r - 1] = cbuf_ref[r:r + span, :]
    blocks = []
    for cb in range(D // LANES):
        cs = slice(cb * LANES, (cb + 1) * LANES)
        acc = None
        for j in range(CONV_WIDTH):
            a, r = divmod(off + j, SUBLANES)
            src = cbuf_ref if r == 0 else shift_ref.at[r - 1]
            term = src[a * SUBLANES:a * SUBLANES + tm, cs] * dw_ref[j:j + 1, cs]
            acc = term if acc is None else acc + term
        blocks.append(acc)

    y = jnp.concatenate(blocks, axis=-1) + dwb_ref[...]
    mu = jnp.mean(y, axis=-1, keepdims=True)
    yc = y - mu
    var = jnp.mean(yc * yc, axis=-1, keepdims=True)
    yn = yc * lax.rsqrt(var + EPS) * lng_ref[...] + lnb_ref[...]
    act = yn * jax.nn.sigmoid(yn)
    o_ref[...] = hprev_ref[...] + _bdot(act.astype(BF16), wout_ref[...])

    cbuf_ref[0:CONV_HIST, :] = cbuf_ref[tm:tm + CONV_HIST, :]
    cbuf_ref[CONV_HIST:CONV_HIST + tm, :] = gnext_ref[...]


def _conv(h2, g, win, bin_, dw, dwb, lng, lnb, wout, *, T):
    N, D = h2.shape
    tm = CONV_TILE
    tps = T // tm
    consts = (g, win, bin_, dw, dwb, lng, lnb, wout)
    n_tiles = N // tm
    prev_tile = lambda i: (jnp.maximum(i - 1, 0), 0)
    return pl.pallas_call(
        functools.partial(_conv_kernel, tiles_per_seq=tps),
        grid=(n_tiles + 1,),
        in_specs=[pl.BlockSpec((tm, D), lambda i: (jnp.minimum(i, n_tiles - 1), 0)),
                  pl.BlockSpec((tm, D), prev_tile)] + [_resident(c.shape) for c in consts],
        out_specs=pl.BlockSpec((tm, D), prev_tile),
        out_shape=jax.ShapeDtypeStruct((N, D), F32),
        scratch_shapes=[
            pltpu.VMEM((CONV_HIST + tm, D), F32),
            pltpu.VMEM((SUBLANES - 1, tm + CONV_HIST - SUBLANES, D), F32),
            pltpu.VMEM((tm, D), F32),
        ],
        compiler_params=_params(1),
        name="conv",
    )(h2, h2, *consts)


def _rope_tables(T, rot):
    half = rot // 2
    inv = ROPE_THETA ** (-jnp.arange(half, dtype=F32) * 2.0 / rot)
    ang = jnp.arange(T).astype(F32)[:, None] * inv[None, :]
    return jnp.cos(ang).T, jnp.sin(ang).T


def kernel(x, mem, norm_mix, norm_cross, norm_mem, norm_ffn, w_in_ab, a_q_norm, a_k_norm, pool_w, pool_scale, w_out_ab, conv_w_in, conv_b_in, conv_dw_w, conv_dw_b, conv_ln_g, conv_ln_b, conv_w_out, cross_wq, cross_wk, cross_wv, cross_q_norm, cross_k_norm, cross_wo, ffn_w_gate, ffn_w_up, ffn_w_down):
    B, T, D = x.shape
    N = B * T
    depth = norm_mix.shape[0]
    aw = A_HEADS * HEAD_DIM
    iqw = IDX_HEADS * IDX_DIM
    pw = len(POOL_WINDOWS) * POOL_GROUP
    assert T % ROW_TILE == 0 and T % KEY_CHUNK == 0 and KEY_CHUNK % QUERY_BLOCK == 0
    assert T % CONV_TILE == 0 and CONV_TILE >= CONV_HIST
    row = lambda v: v.reshape(1, -1)

    kT_mem, v_mem = _mem_kv(mem, norm_mem[:, None, :], cross_wk.astype(BF16),
                            cross_wv.astype(BF16), cross_k_norm[:, None, :])
    cos16, sin16 = _rope_tables(T, HEAD_DIM // 4)
    cos8, sin8 = _rope_tables(T, IDX_DIM // 4)

    h = x.reshape(N, D)
    for l in range(depth):
        if l % 2 == 0:
            e = l // 2
            w_in = w_in_ab[e]
            o_u, o_iq, o_iw, o_ik = 3 * aw, 3 * aw + pw, 3 * aw + pw + iqw, 3 * aw + pw + iqw + IDX_HEADS
            wt = jnp.concatenate(
                [w_in[:, :3 * aw], w_in[:, o_iq:o_iw], w_in[:, o_ik:], w_in[:, o_iw:o_ik]],
                axis=1).T.astype(BF16)
            wu = w_in[:, o_u:o_iq].astype(BF16)
            qT, k, vT, iqT, ik, iwT, b2 = _front(
                h, row(norm_mix[l]), wt, wu, a_q_norm[e][:, None], a_k_norm[e][:, None],
                cos16, sin16, cos8, sin8, pool_w[e].astype(BF16), row(pool_scale[e]), B=B, T=T)
            a = _dsa(qT, k, vT, iqT, ik, iwT, B=B, T=T)
            w_out = w_out_ab[e].astype(BF16)
            stream, mix_weights = (h, a.reshape(N, aw), b2), (w_out[:aw], w_out[aw:])
        else:
            o = l // 2
            h = _conv(h, row(norm_mix[l]), conv_w_in[o].astype(BF16), row(conv_b_in[o]),
                      conv_dw_w[o], row(conv_dw_b[o]), row(conv_ln_g[o]), row(conv_ln_b[o]),
                      conv_w_out[o].astype(BF16), T=T)
            stream, mix_weights = (h,), ()
        h = _cross(stream, mix_weights, row(norm_cross[l]), cross_wq[l].astype(BF16),
                   row(cross_q_norm[l]), kT_mem[l], v_mem[l], cross_wo[l].astype(BF16), T=T)
        h = _ffn(h, row(norm_ffn[l]), ffn_w_gate[l].astype(BF16), ffn_w_up[l].astype(BF16),
                 ffn_w_down[l].astype(BF16))
    return h.reshape(B, T, D)
```

```python
import functools

import jax
import jax.numpy as jnp
from jax import lax
from jax.experimental import pallas as pl
from jax.experimental.pallas import tpu as pltpu

F32 = jnp.float32
BF16 = jnp.bfloat16

EPS = 1e-6
A_HEADS = 4
HEAD_DIM = 128
IDX_HEADS = 16
IDX_DIM = 64
TOPK_MAX = 256
ROPE_THETA = 500000.0
POOL_WINDOWS = (2, 4, 8, 16)
POOL_GROUP = 128
POOL_HIST = 16
C_HEADS = 4
CONV_WIDTH = 31
CONV_HIST = 32
SUBLANES = 8
LANES = 128
BF16_ROWS = 16

VMEM_LIMIT_BYTES = 56 * 1024 * 1024
QUERY_BLOCK = 256
KEY_CHUNK = 512
CONV_TILE = 512
ROW_TILE = 512
FFN_CHUNKS = 2
INT32_MIN = -2 ** 31
MASKED_LOGIT = -1e30
V_PAD = 16
V_ROWS = HEAD_DIM + V_PAD
LOG2_E = 1.4426950408889634
KEY_BITS = 32
LOW_KEY_BITS = 16
LOW_BITS_UNCHECKED = 9
INT32_MAX = 2 ** 31 - 1
HIGH_HALF_MASK = -(1 << LOW_KEY_BITS)
SELECT_ALL_THRESHOLD = -3.0e38


def _params(n_axes):
    return pltpu.CompilerParams(
        dimension_semantics=("arbitrary",) * n_axes,
        vmem_limit_bytes=VMEM_LIMIT_BYTES)


def _resident(shape):
    zeros = (0,) * len(shape)
    return pl.BlockSpec(shape, lambda *_: zeros, pipeline_mode=pl.Buffered(1))


def _rms_rows(x, g):
    ms = jnp.mean(x * x, axis=-1, keepdims=True)
    return x * lax.rsqrt(ms + EPS) * g


def _bdot(a, b):
    return jnp.dot(a, b, preferred_element_type=F32)


def _rope_cols(y, cos, sin, half):
    y1, y2 = y[0:half], y[half:2 * half]
    return jnp.concatenate(
        [y1 * cos - y2 * sin, y2 * cos + y1 * sin, y[2 * half:]], axis=0)


def _front_kernel(x_ref, g_ref, wt_ref, wu_ref, gq_ref, gk_ref, cos16_ref, sin16_ref,
                  cos8_ref, sin8_ref, poolw_ref, pscale_ref,
                  qT_ref, k_ref, vT_ref, iqT_ref, ik_ref, iwT_ref, b_ref, ubuf_ref,
                  *, tiles_per_seq):
    tm = x_ref.shape[0]
    i = pl.program_id(0)
    aw = A_HEADS * HEAD_DIM
    iqw = IDX_HEADS * IDX_DIM

    xn = _rms_rows(x_ref[...], g_ref[...])
    xb = xn.astype(BF16)
    xT = xn.T.astype(BF16)
    cols = _bdot(wt_ref[...], xT)

    cos16, sin16 = cos16_ref[...], sin16_ref[...]
    cos8, sin8 = cos8_ref[...], sin8_ref[...]

    def qk_heads(base, gain):
        heads = []
        for h in range(A_HEADS):
            xh = cols[base + h * HEAD_DIM: base + (h + 1) * HEAD_DIM]
            ms = jnp.sum(xh * xh, axis=0, keepdims=True) * (1.0 / HEAD_DIM)
            y = xh * lax.rsqrt(ms + EPS) * gain
            heads.append(_rope_cols(y, cos16, sin16, HEAD_DIM // 8))
        return jnp.concatenate(heads, axis=0)

    qT_ref[0] = (qk_heads(0, gq_ref[...]) * (HEAD_DIM ** -0.5 * LOG2_E)).astype(BF16)
    k_ref[0] = qk_heads(aw, gk_ref[...]).T.astype(BF16)
    ones_blk = (lax.broadcasted_iota(jnp.int32, (V_PAD, tm), 0) == 0).astype(F32)
    v_rows = []
    for h in range(A_HEADS):
        v_rows += [cols[2 * aw + h * HEAD_DIM:2 * aw + (h + 1) * HEAD_DIM], ones_blk]
    vT_ref[0, 0] = jnp.concatenate(v_rows, axis=0).astype(BF16)

    base = 3 * aw
    iq_heads = [
        _rope_cols(cols[base + h * IDX_DIM: base + (h + 1) * IDX_DIM], cos8, sin8, IDX_DIM // 8)
        for h in range(IDX_HEADS)]
    iqT_ref[0] = jnp.concatenate(iq_heads, axis=0).astype(BF16)

    base = 3 * aw + iqw
    ikT = _rope_cols(cols[base:base + IDX_DIM], cos8, sin8, IDX_DIM // 8)
    ikT = jnp.concatenate([ikT, jnp.zeros_like(ikT)], axis=0)
    ik_ref[0] = ikT.T[:, :IDX_DIM].astype(BF16)

    base = 3 * aw + iqw + IDX_DIM
    iwT_ref[0] = cols[base:base + IDX_HEADS] * (IDX_HEADS ** -0.5 * IDX_DIM ** -0.5)

    u = _bdot(xb, wu_ref[...])
    t_in_seq = (i % tiles_per_seq) * tm

    @pl.when(i % tiles_per_seq == 0)
    def _():
        ubuf_ref[0:POOL_HIST, :] = jnp.zeros((POOL_HIST, u.shape[1]), F32)

    ubuf_ref[POOL_HIST:POOL_HIST + tm, :] = u
    t1 = (lax.broadcasted_iota(jnp.int32, (tm, 1), 0) + t_in_seq + 1).astype(F32)
    outs = []
    for g, w in enumerate(POOL_WINDOWS):
        cs = slice(g * POOL_GROUP, (g + 1) * POOL_GROUP)
        acc = u[:, cs]
        for j in range(1, w):
            acc = acc + ubuf_ref[POOL_HIST - j:POOL_HIST - j + tm, cs]
        pooled = acc / jnp.minimum(t1, float(w)) - u[:, cs]
        outs.append(_bdot(pooled.astype(BF16), poolw_ref[g]))
    b_ref[...] = (jnp.concatenate(outs, axis=-1) * pscale_ref[...]).astype(BF16)
    ubuf_ref[0:POOL_HIST, :] = u[tm - POOL_HIST:tm, :]


def _front(x2, g, wt, wu, gq, gk, cos16, sin16, cos8, sin8, poolw, pscale, *, B, T):
    N, D = x2.shape
    tm = KEY_CHUNK
    tps = T // tm
    aw = A_HEADS * HEAD_DIM
    iqw = IDX_HEADS * IDX_DIM
    pw = len(POOL_WINDOWS) * POOL_GROUP
    seq_map = lambda i: (i // tps, 0, i % tps)
    out_shape = (
        jax.ShapeDtypeStruct((B, aw, T), BF16),
        jax.ShapeDtypeStruct((B, T, aw), BF16),
        jax.ShapeDtypeStruct((B, tps, A_HEADS * V_ROWS, tm), BF16),
        jax.ShapeDtypeStruct((B, iqw, T), BF16),
        jax.ShapeDtypeStruct((B, T, IDX_DIM), BF16),
        jax.ShapeDtypeStruct((B, IDX_HEADS, T), F32),
        jax.ShapeDtypeStruct((N, pw), BF16),
    )
    return pl.pallas_call(
        functools.partial(_front_kernel, tiles_per_seq=tps),
        grid=(N // tm,),
        in_specs=[
            pl.BlockSpec((tm, D), lambda i: (i, 0)),
            _resident(g.shape), _resident(wt.shape), _resident(wu.shape),
            _resident(gq.shape), _resident(gk.shape),
            pl.BlockSpec((HEAD_DIM // 8, tm), lambda i: (0, i % tps)),
            pl.BlockSpec((HEAD_DIM // 8, tm), lambda i: (0, i % tps)),
            pl.BlockSpec((IDX_DIM // 8, tm), lambda i: (0, i % tps)),
            pl.BlockSpec((IDX_DIM // 8, tm), lambda i: (0, i % tps)),
            _resident(poolw.shape), _resident(pscale.shape),
        ],
        out_specs=(
            pl.BlockSpec((1, aw, tm), seq_map),
            pl.BlockSpec((1, tm, aw), lambda i: (i // tps, i % tps, 0)),
            pl.BlockSpec((1, 1, A_HEADS * V_ROWS, tm), lambda i: (i // tps, i % tps, 0, 0)),
            pl.BlockSpec((1, iqw, tm), seq_map),
            pl.BlockSpec((1, tm, IDX_DIM), lambda i: (i // tps, i % tps, 0)),
            pl.BlockSpec((1, IDX_HEADS, tm), seq_map),
            pl.BlockSpec((tm, pw), lambda i: (i, 0)),
        ),
        out_shape=out_shape,
        scratch_shapes=[pltpu.VMEM((POOL_HIST + tm, pw), F32)],
        compiler_params=_params(1),
        name="front",
    )(x2, g, wt, wu, gq, gk, cos16, sin16, cos8, sin8, poolw, pscale)


def _key_to_f32(c):
    bits = c ^ ((c >> (KEY_BITS - 1)) & INT32_MAX)
    return lax.bitcast_convert_type(bits, F32)


def _high_half(x):
    bits = lax.bitcast_convert_type(x, jnp.int32) & jnp.int32(HIGH_HALF_MASK)
    return lax.bitcast_convert_type(bits, F32).astype(BF16)


def _dsa_kernel(iqT_ref, iwT_ref, qT_ref, ik_ref, k_ref, vT_ref, o_ref,
                s_ref, s16_ref, acc_ref, lg_ref, *, topk):
    qb, kc = QUERY_BLOCK, KEY_CHUNK
    row_bits = (s_ref.shape[0] - 1).bit_length()
    qi = pl.program_id(1)
    nblk = ((qi + 1) * qb + kc - 1) // kc

    row_i = lax.broadcasted_iota(jnp.int32, (kc, qb), 0)
    col_i = lax.broadcasted_iota(jnp.int32, (kc, qb), 1)

    def score_body(j, carry):
        s0 = pl.multiple_of(j * kc, kc)
        lhs = ik_ref[0, pl.ds(s0, kc), :]
        for h in range(IDX_HEADS):
            rhs = iqT_ref[0, h * IDX_DIM:(h + 1) * IDX_DIM, :]
            val = jnp.maximum(_bdot(lhs, rhs), 0.0) * iwT_ref[0, h:h + 1, :]
            if h == 0:
                val = jnp.where(row_i + s0 <= col_i + qi * qb, val, -jnp.inf)
            else:
                val = s_ref[pl.ds(s0, kc), :] + val
            if h == IDX_HEADS - 1:
                s16_ref[pl.ds(s0, kc), :] = _high_half(val)
            s_ref[pl.ds(s0, kc), :] = val
        return carry

    lax.fori_loop(0, nblk, score_body, 0)

    def count_ge_high(cand16):
        def body(j, acc):
            s0 = pl.multiple_of(j * kc, kc)
            blk = s16_ref[pl.ds(s0, kc), :]
            one = jnp.ones((), BF16)
            parts = [jnp.zeros((BF16_ROWS, qb), BF16) for _ in range(4)]
            for r in range(kc // BF16_ROWS):
                a = parts[r % 4]
                parts[r % 4] = jnp.where(blk[BF16_ROWS * r:BF16_ROWS * (r + 1)] >= cand16, a + one, a)
            part = (parts[0] + parts[1]) + (parts[2] + parts[3])
            return acc + part.astype(F32)

        acc = lax.fori_loop(0, nblk, body, jnp.zeros((BF16_ROWS, qb), F32))
        return jnp.sum(acc, axis=0, keepdims=True)

    def count_ge(cand):
        def body(j, accs):
            s0 = pl.multiple_of(j * kc, kc)
            blk = s_ref[pl.ds(s0, kc), :]
            accs = list(accs)
            for r in range(kc // SUBLANES):
                a = accs[r % len(accs)]
                accs[r % len(accs)] = jnp.where(blk[SUBLANES * r:SUBLANES * (r + 1)] >= cand, a + 1.0, a)
            return tuple(accs)

        accs = lax.fori_loop(0, nblk, body, tuple(jnp.zeros((SUBLANES, qb), F32) for _ in range(4)))
        return jnp.sum(accs[0] + accs[1] + accs[2] + accs[3], axis=0, keepdims=True)

    kf = float(topk)
    def step(c, settled, trial, cnt):
        c = jnp.where(jnp.logical_and(cnt >= kf, settled == 0.0), trial, c)
        return c, jnp.where(cnt == kf, 1.0, settled)

    c, settled = step(jnp.full((1, qb), INT32_MIN, jnp.int32), jnp.zeros((1, qb), F32),
                      jnp.zeros((1, qb), jnp.int32), count_ge_high(jnp.zeros((1, qb), BF16)))

    def high_bit_body(i, state):
        c, settled = state
        trial = c + jnp.left_shift(jnp.int32(1), KEY_BITS - 2 - i)
        return step(c, settled, trial, count_ge_high(_high_half(_key_to_f32(trial))))

    c, settled = lax.fori_loop(0, KEY_BITS - 1 - LOW_KEY_BITS, high_bit_body, (c, settled))

    def low_bit_step(i, c, settled):
        trial = c + jnp.left_shift(jnp.int32(1), LOW_KEY_BITS - 1 - i)
        return step(c, settled, trial, count_ge(_key_to_f32(trial)))

    c, settled = lax.fori_loop(
        0, LOW_BITS_UNCHECKED, lambda i, st: low_bit_step(i, *st), (c, settled))

    def low_bit_cond(state):
        i, _, _, unsettled = state
        return jnp.logical_and(i < LOW_KEY_BITS, unsettled > 0.0)

    def low_bit_body(state):
        i, c, settled, _ = state
        c, settled = low_bit_step(i, c, settled)
        return i + 1, c, settled, jnp.max(1.0 - settled)

    _, c, settled, unsettled = lax.while_loop(
        low_bit_cond, low_bit_body,
        (jnp.int32(LOW_BITS_UNCHECKED), c, settled, jnp.max(1.0 - settled)))
    thr = jnp.where(c == INT32_MIN, SELECT_ALL_THRESHOLD, _key_to_f32(c))

    tied = jnp.logical_and(settled == 0.0, c != INT32_MIN)

    @pl.when(unsettled > 0.0)
    def _():
        def count_rows(pred):
            def body(j, acc):
                s0 = pl.multiple_of(j * kc, kc)
                m = jnp.where(pred(s_ref[pl.ds(s0, kc), :], row_i + s0), 1.0, 0.0)
                return acc + jnp.sum(m, axis=0, keepdims=True)

            return lax.fori_loop(0, nblk, body, jnp.zeros((1, qb), F32))

        need = kf - count_rows(lambda blk, rows: blk > thr)

        def row_bit_body(i, r):
            trial = r + jnp.left_shift(jnp.int32(1), row_bits - 1 - i)
            before = count_rows(lambda blk, rows: jnp.logical_and(blk == thr, rows < trial))
            return jnp.where(before < need, trial, r)

        last = lax.fori_loop(0, row_bits, row_bit_body, jnp.zeros((1, qb), jnp.int32))

        def demote_body(j, carry):
            s0 = pl.multiple_of(j * kc, kc)
            blk = s_ref[pl.ds(s0, kc), :]
            drop = jnp.logical_and(jnp.logical_and(blk == thr, row_i + s0 > last), tied)
            s_ref[pl.ds(s0, kc), :] = jnp.where(drop, -jnp.inf, blk)
            return carry

        lax.fori_loop(0, nblk, demote_body, 0)

    acc_ref[...] = jnp.zeros(acc_ref.shape, F32)

    def logits(jn):
        s0 = pl.multiple_of(jn * kc, kc)
        return [_bdot(k_ref[0, pl.ds(s0, kc), h * HEAD_DIM:(h + 1) * HEAD_DIM],
                      qT_ref[0, h * HEAD_DIM:(h + 1) * HEAD_DIM, :]) for h in range(A_HEADS)]

    def stage_logits(lgs):
        for h in range(A_HEADS):
            lg_ref[h * kc:(h + 1) * kc, :] = lgs[h]

    stage_logits(logits(0))

    def attn_step(j, ms, stage_next):
        if stage_next:
            next_lgs = logits(j + 1)
        s0 = pl.multiple_of(j * kc, kc)
        bias = jnp.where(s_ref[pl.ds(s0, kc), :] >= thr, 0.0, MASKED_LOGIT)
        new_ms = []
        for h in range(A_HEADS):
            vr = slice(h * V_ROWS, (h + 1) * V_ROWS)
            lg = lg_ref[h * kc:(h + 1) * kc, :] + bias
            m_new = jnp.maximum(ms[h], jnp.max(lg, axis=0, keepdims=True))
            p = jnp.exp2(lg - m_new).astype(BF16)
            acc_ref[vr, :] = jnp.exp2(ms[h] - m_new) * acc_ref[vr, :] + _bdot(vT_ref[0, j, vr, :], p)
            new_ms.append(m_new)
        if stage_next:
            stage_logits(next_lgs)
        return tuple(new_ms)

    ms = lax.fori_loop(0, nblk - 1, lambda j, ms: attn_step(j, ms, True),
                       tuple(jnp.full((1, qb), MASKED_LOGIT, F32) for _ in range(A_HEADS)))
    attn_step(nblk - 1, ms, False)
    outs = []
    for h in range(A_HEADS):
        denom = acc_ref[h * V_ROWS + HEAD_DIM:h * V_ROWS + HEAD_DIM + 1, :]
        outs.append(acc_ref[h * V_ROWS:h * V_ROWS + HEAD_DIM, :] * (1.0 / denom))
    o_ref[0] = jnp.concatenate(outs, axis=0).T.astype(BF16)


def _dsa(qT, k, vT, iqT, ik, iwT, *, B, T):
    qb, kc = QUERY_BLOCK, KEY_CHUNK
    aw = A_HEADS * HEAD_DIM
    iqw = IDX_HEADS * IDX_DIM
    topk = min(TOPK_MAX, T // 4)
    blk = lambda b, q: (b, 0, q)
    per_batch = lambda shape: pl.BlockSpec(
        (1,) + shape, lambda b, q: (b,) + (0,) * len(shape), pipeline_mode=pl.Buffered(1))
    return pl.pallas_call(
        functools.partial(_dsa_kernel, topk=topk),
        grid=(B, T // qb),
        in_specs=[
            pl.BlockSpec((1, iqw, qb), blk),
            pl.BlockSpec((1, IDX_HEADS, qb), blk),
            pl.BlockSpec((1, aw, qb), blk),
            per_batch((T, IDX_DIM)),
            per_batch((T, aw)),
            per_batch((T // kc, A_HEADS * V_ROWS, kc)),
        ],
        out_specs=pl.BlockSpec((1, qb, aw), lambda b, q: (b, q, 0)),
        out_shape=jax.ShapeDtypeStruct((B, T, aw), BF16),
        scratch_shapes=[
            pltpu.VMEM((T, qb), F32),
            pltpu.VMEM((T, qb), BF16),
            pltpu.VMEM((A_HEADS * V_ROWS, qb), F32),
            pltpu.VMEM((A_HEADS * kc, qb), F32),
        ],
        compiler_params=_params(2),
        name="dsa",
    )(iqT, iwT, qT, ik, k, vT)


def _mem_kernel(mem_ref, g_ref, wk_ref, wv_ref, gk_ref, kT_ref, v_ref):
    dh = gk_ref.shape[-1]
    mn = _rms_rows(mem_ref[0], g_ref[0]).astype(BF16)
    k = _bdot(mn, wk_ref[0])
    v_ref[0, 0] = _bdot(mn, wv_ref[0]).astype(BF16)
    for c in range(C_HEADS):
        kc = _rms_rows(k[:, c * dh:(c + 1) * dh], gk_ref[0])
        kT_ref[0, 0, c] = kc.T.astype(BF16)


def _mem_kv(mem, g, wk, wv, gk):
    B, M, D = mem.shape
    L = wk.shape[0]
    dh = D // C_HEADS
    lay = lambda *tail: pl.BlockSpec((1,) + tail, lambda l, b: (l,) + (0,) * len(tail))
    return pl.pallas_call(
        _mem_kernel,
        grid=(L, B),
        in_specs=[
            pl.BlockSpec((1, M, D), lambda l, b: (b, 0, 0)),
            lay(1, D), lay(D, D), lay(D, D), lay(1, dh),
        ],
        out_specs=(
            pl.BlockSpec((1, 1, C_HEADS, dh, M), lambda l, b: (l, b, 0, 0, 0)),
            pl.BlockSpec((1, 1, M, D), lambda l, b: (l, b, 0, 0)),
        ),
        out_shape=(
            jax.ShapeDtypeStruct((L, B, C_HEADS, dh, M), BF16),
            jax.ShapeDtypeStruct((L, B, M, D), BF16),
        ),
        compiler_params=_params(2),
        name="mem_kv",
    )(mem, g, wk, wv, gk)


def _cross_kernel(*refs, mix):
    if mix:
        x_ref, a_ref, b_ref, wa_ref, wb_ref = refs[:5]
        h = x_ref[...] + _bdot(a_ref[...], wa_ref[...]) + _bdot(b_ref[...], wb_ref[...])
    else:
        h = refs[0][...]
    g_ref, wq_ref, gq_ref, kT_ref, v_ref, wo_ref, o_ref = refs[-7:]
    dh = gq_ref.shape[-1]
    q = _bdot(_rms_rows(h, g_ref[...]).astype(BF16), wq_ref[...])
    heads = []
    for c in range(C_HEADS):
        cs = slice(c * dh, (c + 1) * dh)
        qc = _rms_rows(q[:, cs], gq_ref[...]).astype(BF16)
        lg = _bdot(qc, kT_ref[0, c]) * dh ** -0.5
        p = jnp.exp(lg - jnp.max(lg, axis=-1, keepdims=True))
        p = p * (1.0 / jnp.sum(p, axis=-1, keepdims=True))
        heads.append(_bdot(p.astype(BF16), v_ref[0, :, cs]))
    o = jnp.concatenate(heads, axis=-1).astype(BF16)
    o_ref[...] = h + _bdot(o, wo_ref[...])


def _cross(stream, mix_weights, g, wq, gq, kT, v, wo, *, T):
    N, D = stream[0].shape
    tm = ROW_TILE
    tps = T // tm
    dh = D // C_HEADS
    M = v.shape[1]
    return pl.pallas_call(
        functools.partial(_cross_kernel, mix=bool(mix_weights)),
        grid=(N // tm,),
        in_specs=[pl.BlockSpec((tm, s.shape[1]), lambda i: (i, 0)) for s in stream]
        + [_resident(w.shape) for w in mix_weights] + [
            _resident(g.shape), _resident(wq.shape), _resident(gq.shape),
            pl.BlockSpec((1, C_HEADS, dh, M), lambda i: (i // tps, 0, 0, 0)),
            pl.BlockSpec((1, M, D), lambda i: (i // tps, 0, 0)),
            _resident(wo.shape),
        ],
        out_specs=pl.BlockSpec((tm, D), lambda i: (i, 0)),
        out_shape=jax.ShapeDtypeStruct((N, D), F32),
        compiler_params=_params(1),
        name="cross",
    )(*stream, *mix_weights, g, wq, gq, kT, v, wo)


def _ffn_kernel(h_ref, g_ref, wg_ref, wu_ref, wd_ref, o_ref, *, n_chunks):
    h = h_ref[...]
    hn = _rms_rows(h, g_ref[...]).astype(BF16)
    fc = wg_ref.shape[1] // n_chunks
    acc = h
    for c in range(n_chunks):
        cs = slice(c * fc, (c + 1) * fc)
        gate = _bdot(hn, wg_ref[:, cs])
        act = gate * jax.nn.sigmoid(gate) * _bdot(hn, wu_ref[:, cs])
        acc = acc + _bdot(act.astype(BF16), wd_ref[cs, :])
    o_ref[...] = acc


def _ffn(h2, g, wg, wu, wd):
    N, D = h2.shape
    tm = ROW_TILE
    return pl.pallas_call(
        functools.partial(_ffn_kernel, n_chunks=FFN_CHUNKS),
        grid=(N // tm,),
        in_specs=[
            pl.BlockSpec((tm, D), lambda i: (i, 0)),
            _resident(g.shape), _resident(wg.shape), _resident(wu.shape), _resident(wd.shape),
        ],
        out_specs=pl.BlockSpec((tm, D), lambda i: (i, 0)),
        out_shape=jax.ShapeDtypeStruct((N, D), F32),
        compiler_params=_params(1),
        name="ffn",
    )(h2, g, wg, wu, wd)


def _conv_kernel(h_ref, hprev_ref, g_ref, win_ref, bin_ref, dw_ref, dwb_ref, lng_ref, lnb_ref,
                 wout_ref, o_ref, cbuf_ref, shift_ref, gnext_ref, *, tiles_per_seq):
    tm, D = h_ref.shape
    i = pl.program_id(0)

    @pl.when(i == 0)
    def _():
        cbuf_ref[...] = jnp.zeros(cbuf_ref.shape, F32)

    @pl.when((i + tiles_per_seq - 1) % tiles_per_seq == 0)
    def _():
        cbuf_ref[0:CONV_HIST, :] = jnp.zeros((CONV_HIST, D), F32)

    hh = _bdot(_rms_rows(h_ref[...], g_ref[...]).astype(BF16), win_ref[...]) + bin_ref[...]
    gnext_ref[...] = hh[:, :D] * jax.nn.sigmoid(hh[:, D:])

    off = CONV_HIST - (CONV_WIDTH - 1)
    span = tm + CONV_HIST - SUBLANES
    for r in range(1, SUBLANES):
        shift_ref[r - 1] = cbuf_ref[r:r + span, :]
    blocks = []
    for cb in range(D // LANES):
        cs = slice(cb * LANES, (cb + 1) * LANES)
        acc = None
        for j in range(CONV_WIDTH):
            a, r = divmod(off + j, SUBLANES)
            src = cbuf_ref if r == 0 else shift_ref.at[r - 1]
            term = src[a * SUBLANES:a * SUBLANES + tm, cs] * dw_ref[j:j + 1, cs]
            acc = term if acc is None else acc + term
        blocks.append(acc)

    y = jnp.concatenate(blocks, axis=-1) + dwb_ref[...]
    mu = jnp.mean(y, axis=-1, keepdims=True)
    yc = y - mu
    var = jnp.mean(yc * yc, axis=-1, keepdims=True)
    yn = yc * lax.rsqrt(var + EPS) * lng_ref[...] + lnb_ref[...]
    act = yn * jax.nn.sigmoid(yn)
    o_ref[...] = hprev_ref[...] + _bdot(act.astype(BF16), wout_ref[...])

    cbuf_ref[0:CONV_HIST, :] = cbuf_ref[tm:tm + CONV_HIST, :]
    cbuf_ref[CONV_HIST:CONV_HIST + tm, :] = gnext_ref[...]


def _conv(h2, g, win, bin_, dw, dwb, lng, lnb, wout, *, T):
    N, D = h2.shape
    tm = CONV_TILE
    tps = T // tm
    consts = (g, win, bin_, dw, dwb, lng, lnb, wout)
    n_tiles = N // tm
    prev_tile = lambda i: (jnp.maximum(i - 1, 0), 0)
    return pl.pallas_call(
        functools.partial(_conv_kernel, tiles_per_seq=tps),
        grid=(n_tiles + 1,),
        in_specs=[pl.BlockSpec((tm, D), lambda i: (jnp.minimum(i, n_tiles - 1), 0)),
                  pl.BlockSpec((tm, D), prev_tile)] + [_resident(c.shape) for c in consts],
        out_specs=pl.BlockSpec((tm, D), prev_tile),
        out_shape=jax.ShapeDtypeStruct((N, D), F32),
        scratch_shapes=[
            pltpu.VMEM((CONV_HIST + tm, D), F32),
            pltpu.VMEM((SUBLANES - 1, tm + CONV_HIST - SUBLANES, D), F32),
            pltpu.VMEM((tm, D), F32),
        ],
        compiler_params=_params(1),
        name="conv",
    )(h2, h2, *consts)


def _rope_tables(T, rot):
    half = rot // 2
    inv = ROPE_THETA ** (-jnp.arange(half, dtype=F32) * 2.0 / rot)
    ang = jnp.arange(T).astype(F32)[:, None] * inv[None, :]
    return jnp.cos(ang).T, jnp.sin(ang).T


def kernel(x, mem, norm_mix, norm_cross, norm_mem, norm_ffn, w_in_ab, a_q_norm, a_k_norm, pool_w, pool_scale, w_out_ab, conv_w_in, conv_b_in, conv_dw_w, conv_dw_b, conv_ln_g, conv_ln_b, conv_w_out, cross_wq, cross_wk, cross_wv, cross_q_norm, cross_k_norm, cross_wo, ffn_w_gate, ffn_w_up, ffn_w_down):
    B, T, D = x.shape
    N = B * T
    depth = norm_mix.shape[0]
    aw = A_HEADS * HEAD_DIM
    iqw = IDX_HEADS * IDX_DIM
    pw = len(POOL_WINDOWS) * POOL_GROUP
    assert T % ROW_TILE == 0 and T % KEY_CHUNK == 0 and KEY_CHUNK % QUERY_BLOCK == 0
    assert T % CONV_TILE == 0 and CONV_TILE >= CONV_HIST
    row = lambda v: v.reshape(1, -1)

    kT_mem, v_mem = _mem_kv(mem, norm_mem[:, None, :], cross_wk.astype(BF16),
                            cross_wv.astype(BF16), cross_k_norm[:, None, :])
    cos16, sin16 = _rope_tables(T, HEAD_DIM // 4)
    cos8, sin8 = _rope_tables(T, IDX_DIM // 4)

    h = x.reshape(N, D)
    for l in range(depth):
        if l % 2 == 0:
            e = l // 2
            w_in = w_in_ab[e]
            o_u, o_iq, o_iw, o_ik = 3 * aw, 3 * aw + pw, 3 * aw + pw + iqw, 3 * aw + pw + iqw + IDX_HEADS
            wt = jnp.concatenate(
                [w_in[:, :3 * aw], w_in[:, o_iq:o_iw], w_in[:, o_ik:], w_in[:, o_iw:o_ik]],
                axis=1).T.astype(BF16)
            wu = w_in[:, o_u:o_iq].astype(BF16)
            qT, k, vT, iqT, ik, iwT, b2 = _front(
                h, row(norm_mix[l]), wt, wu, a_q_norm[e][:, None], a_k_norm[e][:, None],
                cos16, sin16, cos8, sin8, pool_w[e].astype(BF16), row(pool_scale[e]), B=B, T=T)
            a = _dsa(qT, k, vT, iqT, ik, iwT, B=B, T=T)
            w_out = w_out_ab[e].astype(BF16)
            stream, mix_weights = (h, a.reshape(N, aw), b2), (w_out[:aw], w_out[aw:])
        else:
            o = l // 2
            h = _conv(h, row(norm_mix[l]), conv_w_in[o].astype(BF16), row(conv_b_in[o]),
                      conv_dw_w[o], row(conv_dw_b[o]), row(conv_ln_g[o]), row(conv_ln_b[o]),
                      conv_w_out[o].astype(BF16), T=T)
            stream, mix_weights = (h,), ()
        h = _cross(stream, mix_weights, row(norm_cross[l]), cross_wq[l].astype(BF16),
                   row(cross_q_norm[l]), kT_mem[l], v_mem[l], cross_wo[l].astype(BF16), T=T)
        h = _ffn(h, row(norm_ffn[l]), ffn_w_gate[l].astype(BF16), ffn_w_up[l].astype(BF16),
                 ffn_w_down[l].astype(BF16))
    return h.reshape(B, T, D)
```
